```python
import math
import jax
import jax.numpy as jnp
from jax import lax
import numpy as np


D_MODEL = 2048
BATCH = 8
SEQ = 2048
DEPTH = 4

GRID_W = 64
CTX_LEN = 256
HEAD_DIM = 64
BRANCH_W = 512
N_BRANCH = 4
S5_CH = 16
S5_GROUPS = BRANCH_W // S5_CH
S5_STATE = 64
WIN_HEADS = BRANCH_W // HEAD_DIM
WIN_KV_HEADS = 2
WINDOW = 128
BLOCK = 128
DIFF_HEADS = BRANCH_W // (2 * HEAD_DIM)
DIFF_VDIM = 2 * HEAD_DIM
NA_HEADS = BRANCH_W // HEAD_DIM
NA_WIN_R = 8
NA_WIN_C = 16
N_GROUPS = 4
EXPERTS_PER_GROUP = 4
N_EXPERTS = N_GROUPS * EXPERTS_PER_GROUP
TOP_K_IN_GROUP = 2
EXPERT_FF = D_MODEL // 2
ROPE_BASE = 100.0
EPS = 1e-6
NEG_INF = -1e30
IN_SIZES = (BRANCH_W,
            WIN_HEADS * HEAD_DIM, WIN_KV_HEADS * HEAD_DIM, WIN_KV_HEADS * HEAD_DIM,
            2 * DIFF_HEADS * HEAD_DIM, 2 * DIFF_HEADS * HEAD_DIM, DIFF_HEADS * DIFF_VDIM,
            NA_HEADS * HEAD_DIM, NA_HEADS * HEAD_DIM, NA_HEADS * HEAD_DIM,
            N_BRANCH * D_MODEL)
IN_WIDTH = sum(IN_SIZES)

kernel_name = 'hybrid_prefix_dit_block'


def rms_norm(x, gain):
    xf = x.astype(jnp.float32)
    y = xf * lax.rsqrt(jnp.mean(xf * xf, axis=-1, keepdims=True) + EPS)
    return (y * gain.astype(jnp.float32)).astype(x.dtype)


def softmax_f32(s):
    return jax.nn.softmax(s.astype(jnp.float32), axis=-1)


def axial_rope_tables(n_tokens):
    t = jnp.arange(n_tokens, dtype=jnp.int32)
    row = (t // GRID_W).astype(jnp.float32)
    col = (t % GRID_W).astype(jnp.float32)
    per_axis = HEAD_DIM // 4
    inv_freq = ROPE_BASE ** (-jnp.arange(per_axis, dtype=jnp.float32) / per_axis)
    ang = jnp.concatenate([row[:, None] * inv_freq, col[:, None] * inv_freq], axis=-1)
    return jnp.cos(ang), jnp.sin(ang)


def apply_rope(x, cos, sin):
    shape = (x.shape[1],) + (1,) * (x.ndim - 3) + (HEAD_DIM // 2,)
    cos = cos.reshape(shape)
    sin = sin.reshape(shape)
    x1, x2 = jnp.split(x.astype(jnp.float32), 2, axis=-1)
    return jnp.concatenate([x1 * cos - x2 * sin, x1 * sin + x2 * cos], axis=-1).astype(x.dtype)


def _diag_combine(left, right):
    a1, b1 = left
    a2, b2 = right
    return a2 * a1, a2 * b1 + b2


def s5_branch(u, a_re, a_im, log_step, b_re, b_im, c_re, c_im, d_skip, w_glu, b_glu):
    bsz, tlen, _ = u.shape
    f32 = jnp.float32
    uf = u.astype(f32).reshape(bsz, tlen, S5_GROUPS, S5_CH)
    u_rev = jnp.concatenate([uf[:, :CTX_LEN][:, ::-1], uf[:, CTX_LEN:][:, ::-1]], axis=1)
    y = d_skip.astype(f32).reshape(S5_GROUPS, S5_CH) * uf
    for direction, seq in enumerate((uf, u_rev)):
        lam = lax.complex(a_re[direction].astype(f32), a_im[direction].astype(f32))
        dt = jnp.exp(log_step[direction].astype(f32))[:, None]
        lam_bar = jnp.exp(lam * dt)
        b_mat = lax.complex(b_re[direction].astype(f32), b_im[direction].astype(f32))
        b_bar = ((lam_bar - 1.0) / lam)[..., None] * b_mat
        c_mat = lax.complex(c_re[direction].astype(f32), c_im[direction].astype(f32))
        bu = jnp.einsum('gpi,btgi->btgp', b_bar, seq.astype(jnp.complex64))
        a_seq = jnp.broadcast_to(lam_bar, (1, tlen, S5_GROUPS, S5_STATE))
        _, states = lax.associative_scan(_diag_combine, (a_seq, bu), axis=1)
        yd = jnp.einsum('gip,btgp->btgi', c_mat, states).real
        if direction == 1:
            yd = jnp.concatenate([yd[:, :CTX_LEN][:, ::-1], yd[:, CTX_LEN:][:, ::-1]], axis=1)
        y = y + yd
    y = jax.nn.gelu(y.reshape(bsz, tlen, BRANCH_W))
    out = y * jax.nn.sigmoid(y @ w_glu.astype(f32) + b_glu.astype(f32))
    return out.astype(u.dtype)


def window_branch(q, k, v, q_gain, k_gain, sink, cos, sin):
    bsz, tlen, _ = q.shape
    n_lat = tlen - CTX_LEN
    grp = WIN_HEADS // WIN_KV_HEADS
    scale = HEAD_DIM ** -0.5
    q = rms_norm(q.reshape(bsz, tlen, WIN_HEADS, HEAD_DIM), q_gain)
    k = rms_norm(k.reshape(bsz, tlen, WIN_KV_HEADS, HEAD_DIM), k_gain)
    v = v.reshape(bsz, tlen, WIN_KV_HEADS, HEAD_DIM)
    qc, ql = q[:, :CTX_LEN], apply_rope(q[:, CTX_LEN:], cos, sin)
    kc, kl = k[:, :CTX_LEN], apply_rope(k[:, CTX_LEN:], cos, sin)
    vc, vl = v[:, :CTX_LEN], v[:, CTX_LEN:]
    sink_f = sink.astype(jnp.float32).reshape(WIN_KV_HEADS, grp)
    nb = n_lat // BLOCK
    n_shift = 1 + 2 * (WINDOW // BLOCK)
    span = n_shift * BLOCK
    pad = ((0, 0), (WINDOW, WINDOW), (0, 0), (0, 0))
    kp = jnp.pad(kl, pad)
    vp = jnp.pad(vl, pad)
    k_band = jnp.concatenate([kp[:, i * BLOCK:i * BLOCK + n_lat].reshape(bsz, nb, BLOCK, WIN_KV_HEADS, HEAD_DIM) for i in range(n_shift)], axis=2)
    v_band = jnp.concatenate([vp[:, i * BLOCK:i * BLOCK + n_lat].reshape(bsz, nb, BLOCK, WIN_KV_HEADS, HEAD_DIM) for i in range(n_shift)], axis=2)
    qb = ql.reshape(bsz, nb, BLOCK, WIN_KV_HEADS, grp, HEAD_DIM)
    s_loc = jnp.einsum('bnqkgd,bnskd->bnkgqs', qb, k_band).astype(jnp.float32) * scale
    s_ctx = jnp.einsum('bnqkgd,bskd->bnkgqs', qb, kc).astype(jnp.float32) * scale
    blk = jnp.arange(nb)[:, None] * BLOCK
    qpos = blk + jnp.arange(BLOCK)[None, :]
    kpos = blk - WINDOW + jnp.arange(span)[None, :]
    valid = ((jnp.abs(qpos[:, :, None] - kpos[:, None, :]) <= WINDOW)
             & (kpos[:, None, :] >= 0) & (kpos[:, None, :] < n_lat))
    s_loc = jnp.where(valid[None, :, None, None], s_loc, NEG_INF)
    sink_col = jnp.broadcast_to(sink_f[None, None, :, :, None, None], s_loc.shape[:-1] + (1,))
    p = softmax_f32(jnp.concatenate([s_loc, s_ctx, sink_col], axis=-1))
    p_loc = p[..., :span].astype(v.dtype)
    p_ctx = p[..., span:span + CTX_LEN].astype(v.dtype)
    o_lat = (jnp.einsum('bnkgqs,bnskd->bnqkgd', p_loc, v_band)
             + jnp.einsum('bnkgqs,bskd->bnqkgd', p_ctx, vc))
    o_lat = o_lat.reshape(bsz, n_lat, WIN_HEADS * HEAD_DIM)
    qcr = qc.reshape(bsz, CTX_LEN, WIN_KV_HEADS, grp, HEAD_DIM)
    s_cc = jnp.einsum('bqkgd,bskd->bkgqs', qcr, kc).astype(jnp.float32) * scale
    sink_cc = jnp.broadcast_to(sink_f[None, :, :, None, None], s_cc.shape[:-1] + (1,))
    p_cc = softmax_f32(jnp.concatenate([s_cc, sink_cc], axis=-1))[..., :CTX_LEN].astype(v.dtype)
    o_ctx = jnp.einsum('bkgqs,bskd->bqkgd', p_cc, vc).reshape(bsz, CTX_LEN, WIN_HEADS * HEAD_DIM)
    return jnp.concatenate([o_ctx, o_lat], axis=1)


def diff_branch(q, k, v, q_gain, k_gain, lam_params, sub_gain, lam_init, cos, sin):
    bsz, tlen, _ = q.shape
    n_lat = tlen - CTX_LEN
    scale = HEAD_DIM ** -0.5
    q = rms_norm(q.reshape(bsz, tlen, DIFF_HEADS, 2, HEAD_DIM), q_gain)
    k = rms_norm(k.reshape(bsz, tlen, DIFF_HEADS, 2, HEAD_DIM), k_gain)
    v = v.reshape(bsz, tlen, DIFF_HEADS, DIFF_VDIM)
    lp = lam_params.astype(jnp.float32)
    lam = jnp.exp(jnp.sum(lp[0] * lp[1])) - jnp.exp(jnp.sum(lp[2] * lp[3])) + lam_init
    qc, ql = q[:, :CTX_LEN], apply_rope(q[:, CTX_LEN:], cos, sin)
    kc = k[:, :CTX_LEN]
    k_all = jnp.concatenate([kc, apply_rope(k[:, CTX_LEN:], cos, sin)], axis=1)
    vc = v[:, :CTX_LEN]

    def diff_attend(q_blk, keys, values):
        s = jnp.einsum('bqhmd,bkhmd->bhmqk', q_blk, keys).astype(jnp.float32) * scale
        p = softmax_f32(s)
        p_diff = (p[:, :, 0] - lam * p[:, :, 1]).astype(values.dtype)
        return jnp.einsum('bhqk,bkhe->bqhe', p_diff, values)

    nb = n_lat // BLOCK
    q_blocks = jnp.moveaxis(ql.reshape(bsz, nb, BLOCK, DIFF_HEADS, 2, HEAD_DIM), 1, 0)
    o_lat = lax.map(lambda qb: diff_attend(qb, k_all, v), q_blocks)
    o_lat = jnp.moveaxis(o_lat, 0, 1).reshape(bsz, n_lat, DIFF_HEADS, DIFF_VDIM)
    o_ctx = diff_attend(qc, kc, vc)
    o = jnp.concatenate([o_ctx, o_lat], axis=1)
    o = rms_norm(o, sub_gain) * (1.0 - lam_init)
    return o.reshape(bsz, tlen, DIFF_HEADS * DIFF_VDIM)


def neighborhood_branch(q, k, v, q_gain, k_gain, rpb):
    bsz, tlen, _ = q.shape
    n_lat = tlen - CTX_LEN
    rows = n_lat // GRID_W
    win_r = min(NA_WIN_R, rows)
    span = win_r * GRID_W
    scale = HEAD_DIM ** -0.5
    q = rms_norm(q.reshape(bsz, tlen, NA_HEADS, HEAD_DIM), q_gain)
    k = rms_norm(k.reshape(bsz, tlen, NA_HEADS, HEAD_DIM), k_gain)
    v = v.reshape(bsz, tlen, NA_HEADS, HEAD_DIM)
    qc, kc, vc = q[:, :CTX_LEN], k[:, :CTX_LEN], v[:, :CTX_LEN]
    qg = q[:, CTX_LEN:].reshape(bsz, rows, GRID_W, NA_HEADS, HEAD_DIM)
    kg = k[:, CTX_LEN:].reshape(bsz, rows, GRID_W, NA_HEADS, HEAD_DIM)
    vg = v[:, CTX_LEN:].reshape(bsz, rows, GRID_W, NA_HEADS, HEAD_DIM)
    r = jnp.arange(rows)
    row_idx = jnp.clip(r - win_r // 2, 0, rows - win_r)[:, None] + jnp.arange(win_r)[None, :]
    k_rows = kg[:, row_idx].reshape(bsz, rows, span, NA_HEADS, HEAD_DIM)
    v_rows = vg[:, row_idx].reshape(bsz, rows, span, NA_HEADS, HEAD_DIM)
    col = jnp.arange(GRID_W)
    col_start = jnp.clip(col - NA_WIN_C // 2, 0, GRID_W - NA_WIN_C)
    col_ok = (col[None, :] >= col_start[:, None]) & (col[None, :] < col_start[:, None] + NA_WIN_C)
    mask = jnp.broadcast_to(col_ok[:, None, :], (GRID_W, win_r, GRID_W)).reshape(GRID_W, span)
    r_off = row_idx - r[:, None] + (NA_WIN_R - 1)
    c_off = jnp.clip(col[None, :] - col[:, None] + (NA_WIN_C - 1), 0, 2 * NA_WIN_C - 2)
    bias = rpb[:, r_off[:, None, :, None], c_off[None, :, None, :]]
    bias = jnp.moveaxis(bias.reshape(NA_HEADS, rows, GRID_W, span), 0, 1).astype(jnp.float32)
    s_loc = jnp.einsum('brqhd,brkhd->brhqk', qg, k_rows).astype(jnp.float32) * scale + bias[None]
    s_loc = jnp.where(mask[None, None, None], s_loc, NEG_INF)
    s_ctx = jnp.einsum('brqhd,bkhd->brhqk', qg, kc).astype(jnp.float32) * scale
    p = softmax_f32(jnp.concatenate([s_loc, s_ctx], axis=-1))
    o_lat = (jnp.einsum('brhqk,brkhd->brqhd', p[..., :span].astype(v.dtype), v_rows)
             + jnp.einsum('brhqk,bkhd->brqhd', p[..., span:].astype(v.dtype), vc))
    o_lat = o_lat.reshape(bsz, n_lat, NA_HEADS * HEAD_DIM)
    s_cc = jnp.einsum('bqhd,bkhd->bhqk', qc, kc).astype(jnp.float32) * scale
    o_ctx = jnp.einsum('bhqk,bkhd->bqhd', softmax_f32(s_cc).astype(v.dtype), vc).reshape(bsz, CTX_LEN, NA_HEADS * HEAD_DIM)
    return jnp.concatenate([o_ctx, o_lat], axis=1)


def hybrid_mixer(h, w_in, w_branch, w_out,
                 s5_a_re, s5_a_im, s5_log_step, s5_b_re, s5_b_im, s5_c_re, s5_c_im, s5_d, s5_w_glu, s5_b_glu,
                 win_qn, win_kn, win_sink, diff_qn, diff_kn, diff_lambda, diff_subln, lam_init,
                 na_qn, na_kn, na_rpb, cos, sin):
    offs = []
    acc = 0
    for size in IN_SIZES[:-1]:
        acc += size
        offs.append(acc)
    u_a, q_b, k_b, v_b, q_c, k_c, v_c, q_d, k_d, v_d, gates = jnp.split(h @ w_in, offs, axis=-1)
    y_a = s5_branch(u_a, s5_a_re, s5_a_im, s5_log_step, s5_b_re, s5_b_im, s5_c_re, s5_c_im, s5_d, s5_w_glu, s5_b_glu)
    y_b = window_branch(q_b, k_b, v_b, win_qn, win_kn, win_sink, cos, sin)
    y_c = diff_branch(q_c, k_c, v_c, diff_qn, diff_kn, diff_lambda, diff_subln, lam_init, cos, sin)
    y_d = neighborhood_branch(q_d, k_d, v_d, na_qn, na_kn, na_rpb)
    gate_parts = jnp.split(gates, N_BRANCH, axis=-1)
    merged = None
    for i, y in enumerate((y_a, y_b, y_c, y_d)):
        term = jax.nn.sigmoid(gate_parts[i]) * (y @ w_branch[i])
        merged = term if merged is None else merged + term
    return merged @ w_out


def hier_moe(h, w_group, b_group, w_expert, b_expert, w1, w3, w2):
    bsz, tlen, dm = h.shape
    hf = h.reshape(bsz * tlen, dm)
    g_prob = softmax_f32(hf @ w_group + b_group)
    g_w, g_idx = lax.top_k(g_prob, 1)
    e_logits = (hf @ w_expert + b_expert).astype(jnp.float32).reshape(-1, N_GROUPS, EXPERTS_PER_GROUP)
    e_in = jnp.take_along_axis(e_logits, g_idx[:, :, None], axis=1)[:, 0]
    top_v, top_i = lax.top_k(e_in, TOP_K_IN_GROUP)
    w_sel = jax.nn.softmax(top_v, axis=-1) * g_w
    expert_id = g_idx * EXPERTS_PER_GROUP + top_i
    combine = jnp.sum(jax.nn.one_hot(expert_id, N_EXPERTS, dtype=jnp.float32) * w_sel[..., None], axis=1).astype(h.dtype)
    out = None
    for e in range(N_EXPERTS):
        ye = (jax.nn.silu(hf @ w1[e]) * (hf @ w3[e])) @ w2[e]
        term = combine[:, e:e + 1] * ye
        out = term if out is None else out + term
    return out.reshape(bsz, tlen, dm)


def modulated_norm(xc, xl, gain, shift_c, scale_c, shift_l, scale_l):
    hc = rms_norm(xc, gain) * (1.0 + scale_c) + shift_c
    hl = rms_norm(xl, gain) * (1.0 + scale_l[:, None]) + shift_l[:, None]
    return jnp.concatenate([hc, hl], axis=1)


def setup_inputs(seed: int = 0) -> dict:
    key = jax.random.key(seed)
    keys = jax.random.split(key, 38)
    f32 = jnp.float32
    D = D_MODEL

    def nrm(i, shape, scale):
        return scale * jax.random.normal(keys[i], shape, f32)

    n_idx = jnp.arange(S5_STATE, dtype=f32)
    s5_shape = (DEPTH, 2, S5_GROUPS, S5_STATE)
    return {
        'x': nrm(0, (BATCH, SEQ, D), 1.0),
        'c': nrm(1, (BATCH, D), 1.0),
        'ctx': nrm(2, (BATCH, CTX_LEN, D), 1.0),
        'c_ctx': nrm(3, (D,), 1.0),
        'w_ada': nrm(4, (DEPTH, D, 6 * D), 0.5 * D ** -0.5),
        'b_ada': nrm(5, (DEPTH, 6 * D), 0.02),
        'norm_mix': 1.0 + nrm(6, (DEPTH, D), 0.02),
        'norm_ffn': 1.0 + nrm(7, (DEPTH, D), 0.02),
        'w_in': nrm(8, (DEPTH, D, IN_WIDTH), D ** -0.5),
        's5_a_re': -0.5 + nrm(9, s5_shape, 0.01),
        's5_a_im': math.pi * n_idx + nrm(10, s5_shape, 0.01),
        's5_log_step': jax.random.uniform(keys[11], (DEPTH, 2, S5_GROUPS), f32, math.log(1e-3), math.log(1e-1)),
        's5_b_re': nrm(12, (DEPTH, 2, S5_GROUPS, S5_STATE, S5_CH), (2 * S5_CH) ** -0.5),
        's5_b_im': nrm(13, (DEPTH, 2, S5_GROUPS, S5_STATE, S5_CH), (2 * S5_CH) ** -0.5),
        's5_c_re': nrm(14, (DEPTH, 2, S5_GROUPS, S5_CH, S5_STATE), S5_STATE ** -0.5),
        's5_c_im': nrm(15, (DEPTH, 2, S5_GROUPS, S5_CH, S5_STATE), S5_STATE ** -0.5),
        's5_d': nrm(16, (DEPTH, BRANCH_W), 0.5),
        's5_w_glu': nrm(17, (DEPTH, BRANCH_W, BRANCH_W), BRANCH_W ** -0.5),
        's5_b_glu': nrm(18, (DEPTH, BRANCH_W), 0.02),
        'win_qn': 1.0 + nrm(19, (DEPTH, HEAD_DIM), 0.02),
        'win_kn': 1.0 + nrm(20, (DEPTH, HEAD_DIM), 0.02),
        'win_sink': nrm(21, (DEPTH, WIN_HEADS), 0.5),
        'diff_qn': 1.0 + nrm(22, (DEPTH, HEAD_DIM), 0.02),
        'diff_kn': 1.0 + nrm(23, (DEPTH, HEAD_DIM), 0.02),
        'diff_lambda': nrm(24, (DEPTH, 4, HEAD_DIM), 0.1),
        'diff_subln': 1.0 + nrm(25, (DEPTH, DIFF_VDIM), 0.02),
        'na_qn': 1.0 + nrm(26, (DEPTH, HEAD_DIM), 0.02),
        'na_kn': 1.0 + nrm(27, (DEPTH, HEAD_DIM), 0.02),
        'na_rpb': nrm(28, (DEPTH, NA_HEADS, 2 * NA_WIN_R - 1, 2 * NA_WIN_C - 1), 0.1),
        'w_branch': nrm(29, (DEPTH, N_BRANCH, BRANCH_W, D), BRANCH_W ** -0.5),
        'w_out': nrm(30, (DEPTH, D, D), D ** -0.5),
        'moe_w_group': nrm(31, (DEPTH, D, N_GROUPS), D ** -0.5),
        'moe_b_group': nrm(32, (DEPTH, N_GROUPS), 0.01),
        'moe_w_expert': nrm(33, (DEPTH, D, N_EXPERTS), D ** -0.5),
        'moe_b_expert': nrm(34, (DEPTH, N_EXPERTS), 0.01),
        'moe_w1': nrm(35, (DEPTH, N_EXPERTS, D, EXPERT_FF), D ** -0.5),
        'moe_w3': nrm(36, (DEPTH, N_EXPERTS, D, EXPERT_FF), D ** -0.5),
        'moe_w2': nrm(37, (DEPTH, N_EXPERTS, EXPERT_FF, D), EXPERT_FF ** -0.5),
    }


def reference(x, c, ctx, c_ctx, w_ada, b_ada, norm_mix, norm_ffn, w_in,
              s5_a_re, s5_a_im, s5_log_step, s5_b_re, s5_b_im, s5_c_re, s5_c_im, s5_d, s5_w_glu, s5_b_glu,
              win_qn, win_kn, win_sink, diff_qn, diff_kn, diff_lambda, diff_subln,
              na_qn, na_kn, na_rpb, w_branch, w_out,
              moe_w_group, moe_b_group, moe_w_expert, moe_b_expert, moe_w1, moe_w3, moe_w2):
    n_lat = x.shape[1]
    cos, sin = axial_rope_tables(n_lat)
    xc, xl = ctx, x
    for layer in range(DEPTH):
        lam_init = 0.8 - 0.6 * math.exp(-0.3 * layer)
        mod_l = jnp.split(jax.nn.silu(c) @ w_ada[layer] + b_ada[layer], 6, axis=-1)
        mod_c = jnp.split(jax.nn.silu(c_ctx) @ w_ada[layer] + b_ada[layer], 6, axis=-1)
        h = modulated_norm(xc, xl, norm_mix[layer], mod_c[0], mod_c[1], mod_l[0], mod_l[1])
        mix = hybrid_mixer(h, w_in[layer], w_branch[layer], w_out[layer],
                           s5_a_re[layer], s5_a_im[layer], s5_log_step[layer], s5_b_re[layer], s5_b_im[layer],
                           s5_c_re[layer], s5_c_im[layer], s5_d[layer], s5_w_glu[layer], s5_b_glu[layer],
                           win_qn[layer], win_kn[layer], win_sink[layer],
                           diff_qn[layer], diff_kn[layer], diff_lambda[layer], diff_subln[layer], lam_init,
                           na_qn[layer], na_kn[layer], na_rpb[layer], cos, sin)
        xc = xc + mod_c[2] * mix[:, :CTX_LEN]
        xl = xl + mod_l[2][:, None] * mix[:, CTX_LEN:]
        h = modulated_norm(xc, xl, norm_ffn[layer], mod_c[3], mod_c[4], mod_l[3], mod_l[4])
        ff = hier_moe(h, moe_w_group[layer], moe_b_group[layer], moe_w_expert[layer], moe_b_expert[layer],
                      moe_w1[layer], moe_w3[layer], moe_w2[layer])
        xc = xc + mod_c[5] * ff[:, :CTX_LEN]
        xl = xl + mod_l[5][:, None] * ff[:, CTX_LEN:]
    return xl
```

```python
import functools
import math

import jax
import jax.numpy as jnp
from jax import lax
from jax.experimental import pallas as pl
from jax.experimental.pallas import tpu as pltpu

F32 = jnp.float32
BF16 = jnp.bfloat16

D_MODEL = 2048
DEPTH = 4
GRID_W = 64
CTX_LEN = 256
HEAD_DIM = 64
BRANCH_W = 512
N_BRANCH = 4
S5_CH = 16
S5_GROUPS = BRANCH_W // S5_CH
S5_STATE = 64
WIN_HEADS = 8
WIN_KV_HEADS = 2
WINDOW = 128
DIFF_HEADS = 4
NA_HEADS = 8
NA_WIN_R = 8
NA_WIN_C = 16
N_GROUPS = 4
EXPERTS_PER_GROUP = 4
N_EXPERTS = 16
EXPERT_FF = D_MODEL // 2
ROPE_BASE = 100.0
EPS = 1e-6
NEG_INF = -1e30

LANES = 128
SEG_ROWS = 256
S5_CHUNK = 16
ADA_ROWS = 16
P1_WIDTH = 9 * BRANCH_W
MOE_TM = 256
VMEM_LIMIT = 52 * 1024 * 1024


def _cp(sem, vmem=VMEM_LIMIT):
    return pltpu.CompilerParams(dimension_semantics=sem, vmem_limit_bytes=vmem)


def _dot(a, b):
    return jnp.dot(a, b, preferred_element_type=F32)


def _dot_t(a, b):
    return lax.dot_general(a, b, (((1,), (1,)), ((), ())), preferred_element_type=F32)


def _split_bf16(x):
    hi = x.astype(BF16)
    lo = (x - hi.astype(F32)).astype(BF16)
    return hi, lo


def _ada_kernel(c_ref, w_ref, b_ref, o_ref):
    c = c_ref[...]
    a_hi, a_lo = _split_bf16(c * jax.nn.sigmoid(c))
    w_hi, w_lo = _split_bf16(w_ref[0])
    o_ref[0] = _dot(a_hi, w_hi) + _dot(a_lo, w_hi) + _dot(a_hi, w_lo) + b_ref[0]


def _ada_mod(cond, w_ada, b_ada):
    depth, d, n = w_ada.shape
    tn = 1024
    return pl.pallas_call(
        _ada_kernel,
        grid=(depth, n // tn),
        in_specs=[pl.BlockSpec((ADA_ROWS, d), lambda l, j: (0, 0)),
                  pl.BlockSpec((1, d, tn), lambda l, j: (l, 0, j)),
                  pl.BlockSpec((1, 1, tn), lambda l, j: (l, 0, j))],
        out_specs=pl.BlockSpec((1, ADA_ROWS, tn), lambda l, j: (l, 0, j)),
        out_shape=jax.ShapeDtypeStruct((depth, ADA_ROWS, n), F32),
        compiler_params=_cp(("arbitrary", "arbitrary")),
        name="ada_mod",
    )(cond, w_ada, b_ada.reshape(depth, 1, n))


def _mod_norm(x, gain, sh, sc):
    y = x * lax.rsqrt(jnp.mean(x * x, axis=-1, keepdims=True) + EPS) * gain
    return y * (1.0 + sc) + sh


def _norm_kernel(x_ref, g_ref, sh_ref, sc_ref, o_ref):
    o_ref[0] = _mod_norm(x_ref[0], g_ref[...], sh_ref[0], sc_ref[0]).astype(o_ref.dtype)


def _norm_router_kernel(x_ref, g_ref, sh_ref, sc_ref, whi_ref, wlo_ref, rb_ref, o_ref, lg_ref):
    h = _mod_norm(x_ref[0], g_ref[...], sh_ref[0], sc_ref[0])
    o_ref[0] = h
    h_hi, h_lo = _split_bf16(h)
    lg_ref[0] = (_dot(h_hi, whi_ref[...]) + _dot(h_lo, whi_ref[...]) + _dot(h_hi, wlo_ref[...])
                 + rb_ref[...])


def _norm_specs(d, k_shift, k_scale):
    def mod_spec(k):
        return pl.BlockSpec((1, 1, d), lambda b, t: (2 * b + jnp.minimum(t, 1), 0, k))
    return [pl.BlockSpec((1, SEG_ROWS, d), lambda b, t: (b, t, 0)),
            pl.BlockSpec((1, d), lambda b, t: (0, 0)),
            mod_spec(k_shift), mod_spec(k_scale)]


def _norm(xs, gain, modtab, k_shift, k_scale):
    bsz, tlen, d = xs.shape
    return pl.pallas_call(
        _norm_kernel,
        grid=(bsz, tlen // SEG_ROWS),
        in_specs=_norm_specs(d, k_shift, k_scale),
        out_specs=pl.BlockSpec((1, SEG_ROWS, d), lambda b, t: (b, t, 0)),
        out_shape=jax.ShapeDtypeStruct((bsz, tlen, d), BF16),
        compiler_params=_cp(("parallel", "parallel")),
        name="mod_norm",
    )(xs, gain.reshape(1, d), modtab, modtab)


def _norm_router(xs, gain, modtab, k_shift, k_scale, wr_hi, wr_lo, rb):
    bsz, tlen, d = xs.shape
    blk = pl.BlockSpec((1, SEG_ROWS, d), lambda b, t: (b, t, 0))
    const = lambda b, t: (0, 0)
    return pl.pallas_call(
        _norm_router_kernel,
        grid=(bsz, tlen // SEG_ROWS),
        in_specs=_norm_specs(d, k_shift, k_scale) + [
            pl.BlockSpec((d, LANES), const), pl.BlockSpec((d, LANES), const),
            pl.BlockSpec((1, LANES), const)],
        out_specs=[blk, pl.BlockSpec((1, SEG_ROWS, LANES), lambda b, t: (b, t, 0))],
        out_shape=[jax.ShapeDtypeStruct((bsz, tlen, d), F32),
                   jax.ShapeDtypeStruct((bsz, tlen, LANES), F32)],
        compiler_params=_cp(("parallel", "parallel")),
        name="mod_norm_router",
    )(xs, gain.reshape(1, d), modtab, modtab, wr_hi, wr_lo, rb)


def _mm_kernel(a_ref, w_ref, o_ref):
    o_ref[...] = _dot(a_ref[...], w_ref[...]).astype(o_ref.dtype)


def _matmul(a, w, out_dtype, tm=1024, tn=512):
    m, k = a.shape
    n = w.shape[1]
    return pl.pallas_call(
        _mm_kernel,
        grid=(m // tm, n // tn),
        in_specs=[pl.BlockSpec((tm, k), lambda i, j: (i, 0)),
                  pl.BlockSpec((k, tn), lambda i, j: (0, j))],
        out_specs=pl.BlockSpec((tm, tn), lambda i, j: (i, j)),
        out_shape=jax.ShapeDtypeStruct((m, n), out_dtype),
        compiler_params=_cp(("parallel", "parallel")),
        name="matmul",
    )(a, w)


def _seg_mod_row(block, blocks_per_batch):
    b = block // blocks_per_batch
    return 2 * b + jnp.minimum(block % blocks_per_batch, 1)


def _mm_resid_kernel(a_ref, w_ref, x_ref, mod_ref, o_ref, *, blocks_per_batch):
    acc = _dot(a_ref[...], w_ref[...])
    sub = a_ref.shape[0] // SEG_ROWS
    for s in range(sub):
        row = _seg_mod_row(pl.program_id(0) * sub + s, blocks_per_batch)
        rows = slice(s * SEG_ROWS, (s + 1) * SEG_ROWS)
        o_ref[rows, :] = x_ref[rows, :] + mod_ref[row] * acc[rows, :]


def _matmul_resid(a, w, x, modtab, k_mod, blocks_per_batch, tm=1024, tn=512):
    m, k = a.shape
    n = w.shape[1]
    nrow = modtab.shape[0]
    return pl.pallas_call(
        functools.partial(_mm_resid_kernel, blocks_per_batch=blocks_per_batch),
        grid=(m // tm, n // tn),
        in_specs=[pl.BlockSpec((tm, k), lambda i, j: (i, 0)),
                  pl.BlockSpec((k, tn), lambda i, j: (0, j)),
                  pl.BlockSpec((tm, tn), lambda i, j: (i, j)),
                  pl.BlockSpec((nrow, 1, tn), lambda i, j: (0, 0, k_mod * (n // tn) + j))],
        out_specs=pl.BlockSpec((tm, tn), lambda i, j: (i, j)),
        out_shape=jax.ShapeDtypeStruct((m, n), F32),
        compiler_params=_cp(("parallel", "parallel")),
        name="matmul_resid",
    )(a, w, x, modtab)


def _merge_kernel(ya_ref, yb_ref, yc_ref, yd_ref, ga_ref, gb_ref, gc_ref, gd_ref, wb_ref, o_ref):
    acc = None
    for i, (y_ref, g_ref) in enumerate(((ya_ref, ga_ref), (yb_ref, gb_ref),
                                        (yc_ref, gc_ref), (yd_ref, gd_ref))):
        gate = 0.5 * jnp.tanh(0.5 * g_ref[...].astype(F32)) + 0.5
        term = gate * _dot(y_ref[...], wb_ref[i])
        acc = term if acc is None else acc + term
    o_ref[...] = acc.astype(o_ref.dtype)


def _merge(ys, gates, wb, tm=1024, tn=512):
    m, bw = ys[0].shape
    n = wb.shape[2]
    nj = n // tn
    y_spec = pl.BlockSpec((tm, bw), lambda i, j: (i, 0))
    g_specs = [pl.BlockSpec((tm, tn), functools.partial(lambda i, j, q: (i, q * nj + j), q=q))
               for q in range(N_BRANCH)]
    return pl.pallas_call(
        _merge_kernel,
        grid=(m // tm, nj),
        in_specs=[y_spec] * N_BRANCH + g_specs + [pl.BlockSpec((N_BRANCH, bw, tn), lambda i, j: (0, 0, j))],
        out_specs=pl.BlockSpec((tm, tn), lambda i, j: (i, j)),
        out_shape=jax.ShapeDtypeStruct((m, n), BF16),
        compiler_params=_cp(("parallel", "parallel")),
        name="branch_merge",
    )(*ys, gates, gates, gates, gates, wb)


def _s5_kernel(u_ref, toep_ref, bre_ref, bim_ref, cre_ref, cim_ref, lre_ref, lim_ref, y_ref,
               sre, sim, *, nb):
    u = u_ref[0, 0]
    sre[...] = _dot(u, bre_ref[0, 0])
    sim[...] = _dot(u, bim_ref[0, 0])
    lre = jnp.broadcast_to(lre_ref[0, 0], (nb, LANES))
    lim = jnp.broadcast_to(lim_ref[0, 0], (nb, LANES))
    nchunk = u.shape[0] // nb

    def body(c, carry):
        xr, xi = carry
        rows = pl.ds(pl.multiple_of(c * nb, nb), nb)
        s_r = sre[rows, :]
        s_i = sim[rows, :]
        sre[rows, :] = xr
        sim[rows, :] = xi
        return lre * xr - lim * xi + s_r, lre * xi + lim * xr + s_i

    zero = jnp.zeros((nb, LANES), F32)
    lax.fori_loop(0, nchunk, body, (zero, zero))
    carry_in = (_dot(sre[...].astype(BF16), cre_ref[0, 0])
                + _dot(sim[...].astype(BF16), cim_ref[0, 0]))
    half = u.shape[1] // 2
    y_ref[0, 0, :, :half] = _dot(u[:, :half], toep_ref[0, 0, 0]) + carry_in[:, :half]
    y_ref[0, 0, :, half:] = _dot(u[:, half:], toep_ref[0, 0, 1]) + carry_in[:, half:]


def _s5_params(a_re, a_im, log_step, b_re, b_im, c_re, c_im):
    L, G, P, CH = S5_CHUNK, S5_GROUPS, S5_STATE, S5_CH
    lam = lax.complex(a_re, a_im)
    lam_dt = lam * jnp.exp(log_step)[..., None]
    lam_bar = jnp.exp(lam_dt)
    b_bar = ((lam_bar - 1.0) / lam)[..., None] * lax.complex(b_re, b_im)
    c_mat = lax.complex(c_re, c_im)
    tau = jnp.arange(L + 1, dtype=F32)
    pw = jnp.exp(lam_dt[:, None] * tau[None, :, None, None].astype(jnp.complex64))
    hi = lax.Precision.HIGHEST
    kern = jnp.einsum('zgip,ztgp,zgpj->zgtij', c_mat, pw[:, :L], b_bar, precision=hi).real
    kern = jnp.concatenate([kern, jnp.zeros_like(kern[:, :, :1])], axis=2)
    s_idx = jnp.arange(L)[:, None]
    t_idx = jnp.arange(L)[None, :]
    sel = jnp.where(t_idx >= s_idx, t_idx - s_idx, L)
    toep = kern[:, :, sel]
    toep = jnp.transpose(toep, (0, 1, 2, 5, 3, 4)).reshape(2, G // 2, 2, L * CH, L * CH)
    bend = pw[:, :L][:, ::-1][:, :, :, :, None] * b_bar[:, None]
    bend = jnp.transpose(bend, (0, 2, 1, 4, 3)).reshape(2, G // 2, 2, L * CH, P)
    cout = c_mat[:, None] * pw[:, 1:][:, :, :, None, :]
    cout = jnp.transpose(cout, (0, 2, 4, 1, 3)).reshape(2, G // 2, 2, P, L * CH)

    def pair_diag(m):
        z = jnp.zeros_like(m[:, :, 0])
        top = jnp.concatenate([m[:, :, 0], z], axis=-1)
        bot = jnp.concatenate([z, m[:, :, 1]], axis=-1)
        return jnp.concatenate([top, bot], axis=-2)

    lam_l = pw[:, L].reshape(2, G // 2, 1, 2 * P)
    return dict(toep=toep.astype(BF16),
                bre=pair_diag(bend.real).astype(BF16), bim=pair_diag(bend.imag).astype(BF16),
                cre=pair_diag(cout.real).astype(BF16), cim=pair_diag(-cout.imag).astype(BF16),
                lre=lam_l.real, lim=lam_l.imag)


def _s5_scan(u2, prm):
    _, nb, tlen, bw = u2.shape
    L, CH = S5_CHUNK, S5_CH
    npair = S5_GROUPS // 2
    nchunk = tlen // L
    pw_ = 2 * L * CH
    ug = u2.reshape(2, nb, nchunk, L, npair, 2, CH)
    ug = jnp.transpose(ug, (0, 4, 2, 1, 5, 3, 6)).reshape(2, npair, nchunk * nb, pw_)
    rows = nchunk * nb
    idx4 = lambda z, g: (z, g, 0, 0)
    y = pl.pallas_call(
        functools.partial(_s5_kernel, nb=nb),
        grid=(2, npair),
        in_specs=[pl.BlockSpec((1, 1, rows, pw_), idx4),
                  pl.BlockSpec((1, 1, 2, L * CH, L * CH), lambda z, g: (z, g, 0, 0, 0)),
                  pl.BlockSpec((1, 1, pw_, LANES), idx4), pl.BlockSpec((1, 1, pw_, LANES), idx4),
                  pl.BlockSpec((1, 1, LANES, pw_), idx4), pl.BlockSpec((1, 1, LANES, pw_), idx4),
                  pl.BlockSpec((1, 1, 1, LANES), idx4), pl.BlockSpec((1, 1, 1, LANES), idx4)],
        out_specs=pl.BlockSpec((1, 1, rows, pw_), idx4),
        out_shape=jax.ShapeDtypeStruct((2, npair, rows, pw_), F32),
        scratch_shapes=[pltpu.VMEM((rows, LANES), F32), pltpu.VMEM((rows, LANES), F32)],
        compiler_params=_cp(("parallel", "parallel")),
        name="s5_scan",
    )(ug, prm['toep'], prm['bre'], prm['bim'], prm['cre'], prm['cim'], prm['lre'], prm['lim'])
    y = y.reshape(2, npair, nchunk, nb, 2, L, CH)
    return jnp.transpose(y, (0, 3, 2, 5, 1, 4, 6)).reshape(2, nb, tlen, bw)


def _glu_kernel(u_ref, yf_ref, yb_ref, d_ref, w_ref, b_ref, o_ref):
    y = d_ref[...] * u_ref[...].astype(F32)
    y = y + yf_ref[...]
    y = y + yb_ref[...]
    cdf = 0.5 * (1.0 + jnp.tanh(math.sqrt(2.0 / math.pi) * (y + 0.044715 * (y * y * y))))
    g = y * cdf
    z = _dot(g.astype(BF16), w_ref[...]) + b_ref[...]
    o_ref[...] = (g * (0.5 * jnp.tanh(0.5 * z) + 0.5)).astype(o_ref.dtype)


def _s5_glu(p1, yf, yb, d_skip, w_glu, b_glu, tm=1024):
    m = p1.shape[0]
    bw = BRANCH_W
    row = pl.BlockSpec((tm, bw), lambda i: (i, 0))
    const = lambda i: (0, 0)
    return pl.pallas_call(
        _glu_kernel,
        grid=(m // tm,),
        in_specs=[row, row, row, pl.BlockSpec((1, bw), const), pl.BlockSpec((bw, bw), const),
                  pl.BlockSpec((1, bw), const)],
        out_specs=row,
        out_shape=jax.ShapeDtypeStruct((m, bw), BF16),
        compiler_params=_cp(("parallel",)),
        name="s5_glu",
    )(p1, yf, yb, d_skip.reshape(1, bw), w_glu, b_glu.reshape(1, bw))


def _lane_masks(rows):
    lane = lax.broadcasted_iota(jnp.int32, (rows, LANES), 1)
    return lane < HEAD_DIM, (lane % HEAD_DIM) < (HEAD_DIM // 2)


def _head_norm(x, gain, lo):
    ss = x * x
    s_lo = jnp.sum(jnp.where(lo, ss, 0.0), axis=-1, keepdims=True)
    s_hi = jnp.sum(jnp.where(lo, 0.0, ss), axis=-1, keepdims=True)
    ms = jnp.where(lo, s_lo, s_hi) * (1.0 / HEAD_DIM)
    return x * lax.rsqrt(ms + EPS) * gain


def _rope(x, cos, sin_signed, first_half):
    partner = jnp.where(first_half, pltpu.roll(x, LANES - HEAD_DIM // 2, 1),
                        pltpu.roll(x, HEAD_DIM // 2, 1))
    return x * cos + partner * sin_signed


def _prep_slab(x, gain, cos, sin_signed, scale):
    rows = x.shape[0]
    lo, first_half = _lane_masks(rows)
    y = _head_norm(x.astype(F32), gain, lo)
    if cos is not None:
        y = _rope(y, cos, sin_signed, first_half)
    if scale is not None:
        y = y * scale
    return y.astype(BF16)


def _prep_keys(k_ref, kn_ref, gain, cos_ref, sin_ref, nslab):
    tlen = k_ref.shape[1]
    for r0 in range(0, tlen, SEG_ROWS):
        rows = slice(r0, r0 + SEG_ROWS)
        for s in range(nslab):
            cols = slice(s * LANES, (s + 1) * LANES)
            cos = None if cos_ref is None else cos_ref[rows, :]
            sin = None if sin_ref is None else sin_ref[rows, :]
            kn_ref[rows, cols] = _prep_slab(k_ref[0, rows, cols], gain, cos, sin, None)


def _split_heads(q):
    lo, _ = _lane_masks(q.shape[0])
    zero = jnp.zeros_like(q)
    return jnp.where(lo, q, zero), jnp.where(lo, zero, q), lo


SCALE = HEAD_DIM ** -0.5


def _win_kernel(sink_ref, q_ref, k_ref, v_ref, cos_ref, sin_ref, qg_ref, kg_ref, o_ref, kn_ref):
    qi = pl.program_id(1)
    tq = q_ref.shape[1]
    tlen = k_ref.shape[1]
    span = tq + 2 * WINDOW

    @pl.when(qi == 0)
    def _():
        _prep_keys(k_ref, kn_ref, kg_ref[...], cos_ref, sin_ref, 1)

    qrows = pl.ds(pl.multiple_of(qi * tq, tq), tq)
    cos_q = cos_ref[qrows, :]
    sin_q = sin_ref[qrows, :]

    def attend(local):
        if local:
            start = jnp.clip(qi * tq - WINDOW, CTX_LEN, tlen - span)
            krows = pl.ds(pl.multiple_of(start, LANES), span)
            qtok = qi * tq + lax.broadcasted_iota(jnp.int32, (tq, span), 0)
            ktok = start + lax.broadcasted_iota(jnp.int32, (tq, span), 1)
            valid = jnp.abs(qtok - ktok) <= WINDOW
            k_loc = kn_ref[krows, :]
            v_loc = v_ref[0, krows, :]
        k_ctx = kn_ref[0:CTX_LEN, :]
        v_ctx = v_ref[0, 0:CTX_LEN, :]
        for s in range(WIN_HEADS // 2):
            cols = slice(s * LANES, (s + 1) * LANES)
            qn = _prep_slab(q_ref[0, :, cols], qg_ref[...], cos_q, sin_q, SCALE)
            q_lo, q_hi, lo = _split_heads(qn)
            outs = []
            for half, qm in enumerate((q_lo, q_hi)):
                sink = sink_ref[s + half * (WIN_HEADS // 2)]
                s_ctx = _dot_t(qm, k_ctx)
                m = jnp.maximum(jnp.max(s_ctx, axis=-1, keepdims=True), sink)
                if local:
                    s_loc = jnp.where(valid, _dot_t(qm, k_loc), NEG_INF)
                    m = jnp.maximum(m, jnp.max(s_loc, axis=-1, keepdims=True))
                e_ctx = jnp.exp(s_ctx - m)
                den = jnp.sum(e_ctx, axis=-1, keepdims=True) + jnp.exp(sink - m)
                o = _dot(e_ctx.astype(BF16), v_ctx)
                if local:
                    e_loc = jnp.exp(s_loc - m)
                    den = den + jnp.sum(e_loc, axis=-1, keepdims=True)
                    o = o + _dot(e_loc.astype(BF16), v_loc)
                outs.append(o / den)
            o_ref[0, :, cols] = jnp.where(lo, outs[0], outs[1]).astype(o_ref.dtype)

    @pl.when(qi < CTX_LEN // tq)
    def _():
        attend(False)

    @pl.when(qi >= CTX_LEN // tq)
    def _():
        attend(True)


def _win_attn(p1, sink, cos_t, sin_t, qg, kg, tq=128):
    bsz, tlen, _ = p1.shape
    kcol = 4 * BRANCH_W * 2 // LANES
    return pl.pallas_call(
        _win_kernel,
        grid=(bsz, tlen // tq),
        in_specs=[pl.BlockSpec(memory_space=pltpu.SMEM),
                  pl.BlockSpec((1, tq, BRANCH_W), lambda b, i: (b, i, 1)),
                  pl.BlockSpec((1, tlen, LANES), lambda b, i: (b, 0, kcol)),
                  pl.BlockSpec((1, tlen, LANES), lambda b, i: (b, 0, kcol + 1)),
                  pl.BlockSpec((tlen, LANES), lambda b, i: (0, 0)),
                  pl.BlockSpec((tlen, LANES), lambda b, i: (0, 0)),
                  pl.BlockSpec((1, LANES), lambda b, i: (0, 0)),
                  pl.BlockSpec((1, LANES), lambda b, i: (0, 0))],
        out_specs=pl.BlockSpec((1, tq, BRANCH_W), lambda b, i: (b, i, 0)),
        out_shape=jax.ShapeDtypeStruct((bsz, tlen, BRANCH_W), BF16),
        scratch_shapes=[pltpu.VMEM((tlen, LANES), BF16)],
        compiler_params=_cp(("parallel", "arbitrary")),
        name="win_attn",
    )(sink, p1, p1, p1, cos_t, sin_t, qg, kg)


def _diff_kernel(lam_ref, q_ref, k_ref, v_ref, cos_ref, sin_ref, qg_ref, kg_ref, sg_ref, o_ref, kn_ref):
    qi = pl.program_id(1)
    tq = q_ref.shape[1]
    tlen = k_ref.shape[1]
    lam = lam_ref[0]
    out_scale = lam_ref[1]

    @pl.when(qi == 0)
    def _():
        _prep_keys(k_ref, kn_ref, kg_ref[...], cos_ref, sin_ref, DIFF_HEADS)

    qrows = pl.ds(pl.multiple_of(qi * tq, tq), tq)
    cos_q = cos_ref[qrows, :]
    sin_q = sin_ref[qrows, :]

    def attend(nk):
        for h in range(DIFF_HEADS):
            cols = slice(h * LANES, (h + 1) * LANES)
            qn = _prep_slab(q_ref[0, :, cols], qg_ref[...], cos_q, sin_q, SCALE)
            q1, q2, _ = _split_heads(qn)
            keys = kn_ref[0:nk, cols]
            s1 = _dot_t(q1, keys)
            s2 = _dot_t(q2, keys)
            e1 = jnp.exp(s1 - jnp.max(s1, axis=-1, keepdims=True))
            e2 = jnp.exp(s2 - jnp.max(s2, axis=-1, keepdims=True))
            r1 = 1.0 / jnp.sum(e1, axis=-1, keepdims=True)
            r2 = lam / jnp.sum(e2, axis=-1, keepdims=True)
            p = (e1 * r1 - e2 * r2).astype(BF16)
            o = _dot(p, v_ref[0, 0:nk, cols])
            o = o * lax.rsqrt(jnp.mean(o * o, axis=-1, keepdims=True) + EPS) * sg_ref[...]
            o_ref[0, :, cols] = (o * out_scale).astype(o_ref.dtype)

    @pl.when(qi < CTX_LEN // tq)
    def _():
        attend(CTX_LEN)

    @pl.when(qi >= CTX_LEN // tq)
    def _():
        attend(tlen)


def _diff_attn(p1, lam_vec, cos_t, sin_t, qg, kg, sg, tq=256):
    bsz, tlen, _ = p1.shape
    full = lambda c: pl.BlockSpec((1, tlen, BRANCH_W), functools.partial(lambda b, i, c: (b, 0, c), c=c))
    const = lambda b, i: (0, 0)
    return pl.pallas_call(
        _diff_kernel,
        grid=(bsz, tlen // tq),
        in_specs=[pl.BlockSpec(memory_space=pltpu.SMEM),
                  pl.BlockSpec((1, tq, BRANCH_W), lambda b, i: (b, i, 2)),
                  full(3), full(4),
                  pl.BlockSpec((tlen, LANES), const), pl.BlockSpec((tlen, LANES), const),
                  pl.BlockSpec((1, LANES), const), pl.BlockSpec((1, LANES), const),
                  pl.BlockSpec((1, LANES), const)],
        out_specs=pl.BlockSpec((1, tq, BRANCH_W), lambda b, i: (b, i, 0)),
        out_shape=jax.ShapeDtypeStruct((bsz, tlen, BRANCH_W), BF16),
        scratch_shapes=[pltpu.VMEM((tlen, BRANCH_W), BF16)],
        compiler_params=_cp(("parallel", "arbitrary")),
        name="diff_attn",
    )(lam_vec, p1, p1, p1, cos_t, sin_t, qg, kg, sg)


def _na_kernel(q_ref, k_ref, v_ref, bias_ref, qg_ref, kg_ref, o_ref, kn_ref, *, grid_rows):
    qi = pl.program_id(1)
    span = NA_WIN_R * GRID_W

    @pl.when(qi == 0)
    def _():
        _prep_keys(k_ref, kn_ref, kg_ref[...], None, None, NA_HEADS // 2)

    def attend(qrows, krows, local):
        for s in range(NA_HEADS // 2):
            cols = slice(s * LANES, (s + 1) * LANES)
            qn = _prep_slab(q_ref[0, qrows, cols], qg_ref[...], None, None, SCALE)
            q_lo, q_hi, lo = _split_heads(qn)
            k_ctx = kn_ref[0:CTX_LEN, cols]
            v_ctx = v_ref[0, 0:CTX_LEN, cols]
            outs = []
            for half, qm in enumerate((q_lo, q_hi)):
                s_ctx = _dot_t(qm, k_ctx)
                m = jnp.max(s_ctx, axis=-1, keepdims=True)
                if local:
                    s_loc = _dot_t(qm, kn_ref[krows, cols]) + bias_ref[0, 2 * s + half]
                    m = jnp.maximum(m, jnp.max(s_loc, axis=-1, keepdims=True))
                e_ctx = jnp.exp(s_ctx - m)
                den = jnp.sum(e_ctx, axis=-1, keepdims=True)
                o = _dot(e_ctx.astype(BF16), v_ctx)
                if local:
                    e_loc = jnp.exp(s_loc - m)
                    den = den + jnp.sum(e_loc, axis=-1, keepdims=True)
                    o = o + _dot(e_loc.astype(BF16), v_ref[0, krows, cols])
                outs.append(o / den)
            o_ref[0, qrows, cols] = jnp.where(lo, outs[0], outs[1]).astype(o_ref.dtype)

    @pl.when(qi == 0)
    def _():
        attend(slice(0, CTX_LEN), None, False)

    @pl.when(qi > 0)
    def _():
        r = qi - 1
        qrows = pl.ds(pl.multiple_of(CTX_LEN + r * GRID_W, GRID_W), GRID_W)
        k0 = jnp.clip(r - NA_WIN_R // 2, 0, grid_rows - NA_WIN_R)
        krows = pl.ds(pl.multiple_of(CTX_LEN + k0 * GRID_W, GRID_W), span)
        attend(qrows, krows, True)


def _na_bias(rpb, rows):
    win_r = NA_WIN_R
    half = win_r // 2
    r = jnp.concatenate([jnp.arange(half + 1), jnp.arange(rows - half + 1, rows)])
    row_idx = jnp.clip(r - half, 0, rows - win_r)[:, None] + jnp.arange(win_r)[None, :]
    r_off = row_idx - r[:, None] + (NA_WIN_R - 1)
    col = jnp.arange(GRID_W)
    c_off = jnp.clip(col[None, :] - col[:, None] + (NA_WIN_C - 1), 0, 2 * NA_WIN_C - 2)
    bias = rpb[:, r_off[:, None, :, None], c_off[None, :, None, :]]
    col_start = jnp.clip(col - NA_WIN_C // 2, 0, GRID_W - NA_WIN_C)
    col_ok = (col[None, :] >= col_start[:, None]) & (col[None, :] < col_start[:, None] + NA_WIN_C)
    bias = jnp.where(col_ok[None, None, :, None, :], bias.astype(F32), NEG_INF)
    bias = bias.reshape(NA_HEADS, r.shape[0], GRID_W, win_r * GRID_W)
    return jnp.moveaxis(bias, 0, 1)


def _na_attn(p1, bias, qg, kg):
    bsz, tlen, _ = p1.shape
    rows = (tlen - CTX_LEN) // GRID_W
    half = NA_WIN_R // 2
    full = lambda c: pl.BlockSpec((1, tlen, BRANCH_W), functools.partial(lambda b, i, c: (b, 0, c), c=c))
    const = lambda b, i: (0, 0)

    def bias_idx(b, i):
        r = jnp.maximum(i - 1, 0)
        return (jnp.minimum(r, half) + jnp.maximum(r - (rows - half), 0), 0, 0, 0)

    return pl.pallas_call(
        functools.partial(_na_kernel, grid_rows=rows),
        grid=(bsz, rows + 1),
        in_specs=[full(5), full(6), full(7),
                  pl.BlockSpec((1, NA_HEADS, GRID_W, NA_WIN_R * GRID_W), bias_idx),
                  pl.BlockSpec((1, LANES), const), pl.BlockSpec((1, LANES), const)],
        out_specs=pl.BlockSpec((1, tlen, BRANCH_W), lambda b, i: (b, 0, 0)),
        out_shape=jax.ShapeDtypeStruct((bsz, tlen, BRANCH_W), BF16),
        scratch_shapes=[pltpu.VMEM((tlen, BRANCH_W), BF16)],
        compiler_params=_cp(("parallel", "arbitrary")),
        name="na_attn",
    )(p1, p1, p1, bias, qg, kg)


def _gather_rows(idx_ref, base, src_hbm, dst, sem, nrows):
    def body(r, carry):
        tok = idx_ref[base + r]
        pltpu.make_async_copy(src_hbm.at[pl.ds(tok, 1), :], dst.at[pl.ds(r, 1), :], sem).start()
        return carry
    lax.fori_loop(0, nrows, body, 0)


def _gather_wait(src_hbm, dst, sem, nrows):
    pltpu.make_async_copy(src_hbm.at[pl.ds(0, nrows), :], dst, sem).wait()


def _expert_kernel(te_ref, rt_ref, na_ref, h_hbm, rw_ref, w1_ref, w3_ref, w2_ref, o_ref, buf, sem):
    i = pl.program_id(0)
    tm = buf.shape[1]
    slot = i % 2
    nact = na_ref[0]

    @pl.when(i == 0)
    def _():
        _gather_rows(rt_ref, 0, h_hbm, buf.at[0], sem.at[0], tm)

    @pl.when(i + 1 < nact)
    def _():
        _gather_rows(rt_ref, (i + 1) * tm, h_hbm, buf.at[1 - slot], sem.at[1 - slot], tm)

    @pl.when(i < jnp.maximum(nact, 1))
    def _():
        _gather_wait(h_hbm, buf.at[slot], sem.at[slot], tm)
        x = buf[slot].astype(BF16)
        a = _dot(x, w1_ref[0])
        mid = (a * jax.nn.sigmoid(a)) * _dot(x, w3_ref[0])
        o_ref[...] = _dot(mid.astype(BF16), w2_ref[0]) * rw_ref[...]

    @pl.when(i >= jnp.maximum(nact, 1))
    def _():
        o_ref[...] = jnp.zeros_like(o_ref)


def _moe_experts(h, tile_expert, row_token, n_active, row_weight, w1, w3, w2, tm=MOE_TM):
    n, d = h.shape
    rmax = row_token.shape[0]
    ff = w1.shape[2]
    grid_spec = pltpu.PrefetchScalarGridSpec(
        num_scalar_prefetch=3,
        grid=(rmax // tm,),
        in_specs=[pl.BlockSpec(memory_space=pl.ANY),
                  pl.BlockSpec((tm, 1), lambda i, te, rt, na: (i, 0)),
                  pl.BlockSpec((1, d, ff), lambda i, te, rt, na: (te[i], 0, 0)),
                  pl.BlockSpec((1, d, ff), lambda i, te, rt, na: (te[i], 0, 0)),
                  pl.BlockSpec((1, ff, d), lambda i, te, rt, na: (te[i], 0, 0))],
        out_specs=pl.BlockSpec((tm, d), lambda i, te, rt, na: (i, 0)),
        scratch_shapes=[pltpu.VMEM((2, tm, d), F32), pltpu.SemaphoreType.DMA((2,))])
    return pl.pallas_call(
        _expert_kernel,
        grid_spec=grid_spec,
        out_shape=jax.ShapeDtypeStruct((rmax, d), F32),
        compiler_params=_cp(("arbitrary",)),
        name="moe_experts",
    )(tile_expert, row_token, n_active, h, row_weight, w1, w3, w2)


def _combine_kernel(d0_ref, d1_ref, ys_hbm, x_ref, mod_ref, o_ref, buf0, buf1, sem,
                    *, blocks_per_batch):
    i = pl.program_id(0)
    n = pl.num_programs(0)
    tm = x_ref.shape[0]
    slot = i % 2

    def issue(tile, s):
        _gather_rows(d0_ref, tile * tm, ys_hbm, buf0.at[s], sem.at[0, s], tm)
        _gather_rows(d1_ref, tile * tm, ys_hbm, buf1.at[s], sem.at[1, s], tm)

    @pl.when(i == 0)
    def _():
        issue(0, 0)

    @pl.when(i + 1 < n)
    def _():
        issue(i + 1, 1 - slot)

    _gather_wait(ys_hbm, buf0.at[slot], sem.at[0, slot], tm)
    _gather_wait(ys_hbm, buf1.at[slot], sem.at[1, slot], tm)
    row = _seg_mod_row(i, blocks_per_batch)
    o_ref[...] = x_ref[...] + mod_ref[row] * (buf0[slot] + buf1[slot])


def _moe_combine(ys, dest0, dest1, x, modtab, k_mod, blocks_per_batch):
    n, d = x.shape
    tm = SEG_ROWS
    nrow = modtab.shape[0]
    grid_spec = pltpu.PrefetchScalarGridSpec(
        num_scalar_prefetch=2,
        grid=(n // tm,),
        in_specs=[pl.BlockSpec(memory_space=pl.ANY),
                  pl.BlockSpec((tm, d), lambda i, a, b: (i, 0)),
                  pl.BlockSpec((nrow, 1, d), lambda i, a, b: (0, 0, k_mod))],
        out_specs=pl.BlockSpec((tm, d), lambda i, a, b: (i, 0)),
        scratch_shapes=[pltpu.VMEM((2, tm, d), F32), pltpu.VMEM((2, tm, d), F32),
                        pltpu.SemaphoreType.DMA((2, 2))])
    return pl.pallas_call(
        functools.partial(_combine_kernel, blocks_per_batch=blocks_per_batch),
        grid_spec=grid_spec,
        out_shape=jax.ShapeDtypeStruct((n, d), F32),
        compiler_params=_cp(("arbitrary",)),
        name="moe_combine",
    )(dest0, dest1, ys, x, modtab)


def _route(logits, b_group_unused=None, tm=MOE_TM):
    n = logits.shape[0]
    g_prob = jax.nn.softmax(logits[:, :N_GROUPS], axis=-1)
    g_w, g_idx = lax.top_k(g_prob, 1)
    e_logits = logits[:, N_GROUPS:N_GROUPS + N_EXPERTS].reshape(n, N_GROUPS, EXPERTS_PER_GROUP)
    e_in = jnp.take_along_axis(e_logits, g_idx[:, :, None], axis=1)[:, 0]
    top_v, top_i = lax.top_k(e_in, 2)
    w_sel = jax.nn.softmax(top_v, axis=-1) * g_w
    expert = (g_idx * EXPERTS_PER_GROUP + top_i).astype(jnp.int32)

    e_flat = expert.reshape(-1)
    onehot = (e_flat[:, None] == jnp.arange(N_EXPERTS, dtype=jnp.int32)[None, :]).astype(jnp.int32)
    csum = jnp.cumsum(onehot, axis=0)
    rank = jnp.take_along_axis(csum, e_flat[:, None], axis=1)[:, 0] - 1
    counts = csum[-1]
    padded = ((counts + tm - 1) // tm) * tm
    pend = jnp.cumsum(padded)
    pstart = pend - padded
    dest = (pstart[e_flat] + rank).astype(jnp.int32)
    rmax = 2 * n + N_EXPERTS * tm
    tok_flat = jnp.arange(2 * n, dtype=jnp.int32) // 2
    row_token = jnp.zeros((rmax,), jnp.int32).at[dest].set(tok_flat, unique_indices=True)
    row_weight = jnp.zeros((rmax,), F32).at[dest].set(w_sel.reshape(-1), unique_indices=True)
    n_active = (pend[-1] // tm).astype(jnp.int32)
    tile_start = jnp.arange(rmax // tm, dtype=jnp.int32) * tm
    tile_expert = jnp.searchsorted(pend, tile_start, side='right').astype(jnp.int32)
    last_expert = tile_expert[jnp.maximum(n_active - 1, 0)]
    tile_expert = jnp.where(tile_start < pend[-1], tile_expert, last_expert)
    dest2 = dest.reshape(n, 2)
    return (tile_expert, row_token, n_active.reshape(1), row_weight.reshape(rmax, 1),
            dest2[:, 0], dest2[:, 1])


def _rope_tables(n_lat):
    t = jnp.arange(n_lat, dtype=jnp.int32)
    row = (t // GRID_W).astype(F32)
    col = (t % GRID_W).astype(F32)
    per_axis = HEAD_DIM // 4
    inv_freq = ROPE_BASE ** (-jnp.arange(per_axis, dtype=F32) / per_axis)
    ang = jnp.concatenate([row[:, None] * inv_freq, col[:, None] * inv_freq], axis=-1)
    cos = jnp.concatenate([jnp.ones((CTX_LEN, HEAD_DIM // 2), F32), jnp.cos(ang)], axis=0)
    sin = jnp.concatenate([jnp.zeros((CTX_LEN, HEAD_DIM // 2), F32), jnp.sin(ang)], axis=0)
    cos_t = jnp.tile(cos, (1, 4))
    sin_t = jnp.tile(jnp.concatenate([-sin, sin], axis=-1), (1, 2))
    return cos_t, sin_t


def _win_head_perm():
    cols = []
    for s in range(WIN_HEADS // 2):
        for half in range(2):
            h = s + half * (WIN_HEADS // 2)
            cols.extend(range(h * HEAD_DIM, (h + 1) * HEAD_DIM))
    return jnp.asarray(cols, dtype=jnp.int32)


def _p1_weight(w_in_l, perm):
    bw = BRANCH_W
    kvw = WIN_KV_HEADS * HEAD_DIM
    o = 0
    u_a = w_in_l[:, o:o + bw]; o += bw
    q_b = w_in_l[:, o:o + bw][:, perm]; o += bw
    k_b = w_in_l[:, o:o + kvw]; o += kvw
    v_b = w_in_l[:, o:o + kvw]; o += kvw
    rest = w_in_l[:, o:o + 6 * bw]; o += 6 * bw
    pad = jnp.zeros((w_in_l.shape[0], bw - 2 * kvw), w_in_l.dtype)
    p1 = jnp.concatenate([u_a, q_b, rest, k_b, v_b, pad], axis=1)
    return p1.astype(BF16), w_in_l[:, o:].astype(BF16)


def kernel(x, c, ctx, c_ctx, w_ada, b_ada, norm_mix, norm_ffn, w_in, s5_a_re, s5_a_im, s5_log_step, s5_b_re, s5_b_im, s5_c_re, s5_c_im, s5_d, s5_w_glu, s5_b_glu, win_qn, win_kn, win_sink, diff_qn, diff_kn, diff_lambda, diff_subln, na_qn, na_kn, na_rpb, w_branch, w_out, moe_w_group, moe_b_group, moe_w_expert, moe_b_expert, moe_w1, moe_w3, moe_w2):
    bsz, n_lat, d = x.shape
    tlen = CTX_LEN + n_lat
    ntok = bsz * tlen
    bpb = tlen // SEG_ROWS
    depth = w_ada.shape[0]

    cond = jnp.zeros((ADA_ROWS, d), F32).at[:bsz].set(c).at[bsz].set(c_ctx)
    mod = _ada_mod(cond, w_ada, b_ada)
    mod_ctx = jnp.broadcast_to(mod[:, bsz:bsz + 1], (depth, bsz, 6 * d))
    modtab = jnp.stack([mod_ctx, mod[:, :bsz]], axis=2).reshape(depth, 2 * bsz, 1, 6 * d)

    cos_t, sin_t = _rope_tables(n_lat)
    perm = _win_head_perm()
    tile2 = lambda g: jnp.tile(g.astype(F32), 2).reshape(1, LANES)
    xs = jnp.concatenate([ctx, x], axis=1)

    for l in range(depth):
        lam_init = 0.8 - 0.6 * math.exp(-0.3 * l)
        mt = modtab[l]
        w_p1, w_gate = _p1_weight(w_in[l], perm)

        h = _norm(xs, norm_mix[l], mt, 0, 1).reshape(ntok, d)
        p1 = _matmul(h, w_p1, BF16)
        gates = _matmul(h, w_gate, BF16)
        p1_3d = p1.reshape(bsz, tlen, P1_WIDTH)

        u = p1_3d[:, :, :BRANCH_W]
        u_rev = jnp.concatenate([u[:, :CTX_LEN][:, ::-1], u[:, CTX_LEN:][:, ::-1]], axis=1)
        prm = _s5_params(s5_a_re[l], s5_a_im[l], s5_log_step[l], s5_b_re[l], s5_b_im[l],
                         s5_c_re[l], s5_c_im[l])
        yd = _s5_scan(jnp.stack([u, u_rev]), prm)
        yb = jnp.concatenate([yd[1, :, :CTX_LEN][:, ::-1], yd[1, :, CTX_LEN:][:, ::-1]], axis=1)
        y_a = _s5_glu(p1, yd[0].reshape(ntok, BRANCH_W), yb.reshape(ntok, BRANCH_W),
                      s5_d[l], s5_w_glu[l].astype(BF16), s5_b_glu[l])

        y_b = _win_attn(p1_3d, win_sink[l].astype(F32), cos_t, sin_t,
                        tile2(win_qn[l]), tile2(win_kn[l])).reshape(ntok, BRANCH_W)

        lp = diff_lambda[l].astype(F32)
        lam = jnp.exp(jnp.sum(lp[0] * lp[1])) - jnp.exp(jnp.sum(lp[2] * lp[3])) + lam_init
        lam_vec = jnp.stack([lam, jnp.asarray(1.0 - lam_init, F32)])
        y_c = _diff_attn(p1_3d, lam_vec, cos_t, sin_t, tile2(diff_qn[l]), tile2(diff_kn[l]),
                         diff_subln[l].astype(F32).reshape(1, LANES)).reshape(ntok, BRANCH_W)

        y_d = _na_attn(p1_3d, _na_bias(na_rpb[l], n_lat // GRID_W),
                       tile2(na_qn[l]), tile2(na_kn[l])).reshape(ntok, BRANCH_W)

        wb = w_branch[l].at[1].set(w_branch[l][1][perm]).astype(BF16)
        merged = _merge((y_a, y_b, y_c, y_d), gates, wb)
        xs = _matmul_resid(merged, w_out[l].astype(BF16), xs.reshape(ntok, d), mt, 2, bpb)
        xs = xs.reshape(bsz, tlen, d)

        w_r = jnp.zeros((d, LANES), F32).at[:, :N_GROUPS].set(moe_w_group[l])
        w_r = w_r.at[:, N_GROUPS:N_GROUPS + N_EXPERTS].set(moe_w_expert[l])
        rb = jnp.zeros((1, LANES), F32).at[0, :N_GROUPS].set(moe_b_group[l])
        rb = rb.at[0, N_GROUPS:N_GROUPS + N_EXPERTS].set(moe_b_expert[l])
        wr_hi, wr_lo = _split_bf16(w_r)
        hf, logits = _norm_router(xs, norm_ffn[l], mt, 3, 4, wr_hi, wr_lo, rb)
        te, rt, na, rw, d0, d1 = _route(logits.reshape(ntok, LANES))
        ys = _moe_experts(hf.reshape(ntok, d), te, rt, na, rw, moe_w1[l].astype(BF16),
                          moe_w3[l].astype(BF16), moe_w2[l].astype(BF16))
        xs = _moe_combine(ys, d0, d1, xs.reshape(ntok, d), mt, 5, bpb).reshape(bsz, tlen, d)

    return xs[:, CTX_LEN:]
```

```python
import functools
import math

import jax
import jax.numpy as jnp
from jax import lax
from jax.experimental import pallas as pl
from jax.experimental.pallas import tpu as pltpu

F32 = jnp.float32
BF16 = jnp.bfloat16

D_MODEL = 2048
DEPTH = 4
GRID_W = 64
CTX_LEN = 256
HEAD_DIM = 64
BRANCH_W = 512
N_BRANCH = 4
S5_CH = 16
S5_GROUPS = BRANCH_W // S5_CH
S5_STATE = 64
WIN_HEADS = 8
WIN_KV_HEADS = 2
WINDOW = 128
DIFF_HEADS = 4
NA_HEADS = 8
NA_WIN_R = 8
NA_WIN_C = 16
N_GROUPS = 4
EXPERTS_PER_GROUP = 4
N_EXPERTS = 16
EXPERT_FF = D_MODEL // 2
ROPE_BASE = 100.0
EPS = 1e-6
NEG_INF = -1e30

LANES = 128
SEG_ROWS = 256
ADA_ROWS = 16
P1_WIDTH = 9 * BRANCH_W
MOE_TM = 256
VMEM_LIMIT = 52 * 1024 * 1024

S5_CHUNK = 16
S5_PACK_ROWS = 128
S5_NPAIR = S5_GROUPS // 2
S5_PIECE = 2 * S5_CH

LOG2E = 1.4426950408889634
SCALE = HEAD_DIM ** -0.5


def _cp(sem, vmem=VMEM_LIMIT):
    return pltpu.CompilerParams(dimension_semantics=sem, vmem_limit_bytes=vmem)


def _dot(a, b):
    return jnp.dot(a, b, preferred_element_type=F32)


def _dot_t(a, b):
    return lax.dot_general(a, b, (((1,), (1,)), ((), ())), preferred_element_type=F32)


def _split_bf16(x):
    hi = x.astype(BF16)
    lo = (x - hi.astype(F32)).astype(BF16)
    return hi, lo


def _sigmoid(x):
    return 0.5 * jnp.tanh(0.5 * x) + 0.5


def _rowsum(x):
    acc = x[:, :LANES]
    for j in range(1, x.shape[1] // LANES):
        acc = acc + x[:, j * LANES:(j + 1) * LANES]
    return jnp.sum(acc, axis=-1, keepdims=True)


def _ada_kernel(c_ref, w_ref, b_ref, o_ref):
    c = c_ref[...]
    a_hi, a_lo = _split_bf16(c * jax.nn.sigmoid(c))
    w_hi, w_lo = _split_bf16(w_ref[0])
    o_ref[0] = _dot(a_hi, w_hi) + _dot(a_lo, w_hi) + _dot(a_hi, w_lo) + b_ref[0]


def _ada_mod(cond, w_ada, b_ada):
    depth, d, n = w_ada.shape
    tn = 1024
    return pl.pallas_call(
        _ada_kernel,
        grid=(depth, n // tn),
        in_specs=[pl.BlockSpec((ADA_ROWS, d), lambda l, j: (0, 0)),
                  pl.BlockSpec((1, d, tn), lambda l, j: (l, 0, j)),
                  pl.BlockSpec((1, 1, tn), lambda l, j: (l, 0, j))],
        out_specs=pl.BlockSpec((1, ADA_ROWS, tn), lambda l, j: (l, 0, j)),
        out_shape=jax.ShapeDtypeStruct((depth, ADA_ROWS, n), F32),
        compiler_params=_cp(("arbitrary", "arbitrary")),
        name="ada_mod",
    )(cond, w_ada, b_ada.reshape(depth, 1, n))


def _mod_norm(x, gain, sh, sc):
    y = x * lax.rsqrt(jnp.mean(x * x, axis=-1, keepdims=True) + EPS) * gain
    return y * (1.0 + sc) + sh


def _norm_kernel(x_ref, g_ref, sh_ref, sc_ref, o_ref):
    o_ref[0] = _mod_norm(x_ref[0], g_ref[...], sh_ref[0], sc_ref[0]).astype(o_ref.dtype)


def _norm_router_kernel(x_ref, g_ref, sh_ref, sc_ref, whi_ref, wlo_ref, rb_ref, o_ref, lg_ref):
    h = _mod_norm(x_ref[0], g_ref[...], sh_ref[0], sc_ref[0])
    o_ref[0] = h
    h_hi, h_lo = _split_bf16(h)
    lg_ref[0] = (_dot(h_hi, whi_ref[...]) + _dot(h_lo, whi_ref[...]) + _dot(h_hi, wlo_ref[...])
                 + rb_ref[...])


def _norm_specs(d, k_shift, k_scale):
    def mod_spec(k):
        return pl.BlockSpec((1, 1, d), lambda b, t: (2 * b + jnp.minimum(t, 1), 0, k))
    return [pl.BlockSpec((1, SEG_ROWS, d), lambda b, t: (b, t, 0)),
            pl.BlockSpec((1, d), lambda b, t: (0, 0)),
            mod_spec(k_shift), mod_spec(k_scale)]


def _norm(xs, gain, modtab, k_shift, k_scale):
    bsz, tlen, d = xs.shape
    return pl.pallas_call(
        _norm_kernel,
        grid=(bsz, tlen // SEG_ROWS),
        in_specs=_norm_specs(d, k_shift, k_scale),
        out_specs=pl.BlockSpec((1, SEG_ROWS, d), lambda b, t: (b, t, 0)),
        out_shape=jax.ShapeDtypeStruct((bsz, tlen, d), BF16),
        compiler_params=_cp(("parallel", "parallel")),
        name="mod_norm",
    )(xs, gain.reshape(1, d), modtab, modtab)


def _norm_router(xs, gain, modtab, k_shift, k_scale, wr_hi, wr_lo, rb):
    bsz, tlen, d = xs.shape
    blk = pl.BlockSpec((1, SEG_ROWS, d), lambda b, t: (b, t, 0))
    const = lambda b, t: (0, 0)
    return pl.pallas_call(
        _norm_router_kernel,
        grid=(bsz, tlen // SEG_ROWS),
        in_specs=_norm_specs(d, k_shift, k_scale) + [
            pl.BlockSpec((d, LANES), const), pl.BlockSpec((d, LANES), const),
            pl.BlockSpec((1, LANES), const)],
        out_specs=[blk, pl.BlockSpec((1, SEG_ROWS, LANES), lambda b, t: (b, t, 0))],
        out_shape=[jax.ShapeDtypeStruct((bsz, tlen, d), F32),
                   jax.ShapeDtypeStruct((bsz, tlen, LANES), F32)],
        compiler_params=_cp(("parallel", "parallel")),
        name="mod_norm_router",
    )(xs, gain.reshape(1, d), modtab, modtab, wr_hi, wr_lo, rb)


def _mm_kernel(a_ref, w_ref, o_ref):
    o_ref[...] = _dot(a_ref[...], w_ref[0]).astype(o_ref.dtype)


def _matmul(a, w_all, layer, out_dtype, tm=1024, tn=512):
    m, k = a.shape
    n = w_all.shape[2]
    return pl.pallas_call(
        _mm_kernel,
        grid=(m // tm, n // tn),
        in_specs=[pl.BlockSpec((tm, k), lambda i, j: (i, 0)),
                  pl.BlockSpec((1, k, tn), lambda i, j: (layer, 0, j))],
        out_specs=pl.BlockSpec((tm, tn), lambda i, j: (i, j)),
        out_shape=jax.ShapeDtypeStruct((m, n), out_dtype),
        compiler_params=_cp(("parallel", "parallel")),
        name="matmul",
    )(a, w_all)


def _seg_mod_row(block, blocks_per_batch):
    b = block // blocks_per_batch
    return 2 * b + jnp.minimum(block % blocks_per_batch, 1)


def _mm_resid_kernel(a_ref, w_ref, x_ref, mod_ref, o_ref, *, blocks_per_batch):
    acc = _dot(a_ref[...], w_ref[0])
    sub = a_ref.shape[0] // SEG_ROWS
    for s in range(sub):
        row = _seg_mod_row(pl.program_id(0) * sub + s, blocks_per_batch)
        rows = slice(s * SEG_ROWS, (s + 1) * SEG_ROWS)
        o_ref[rows, :] = x_ref[rows, :] + mod_ref[row] * acc[rows, :]


def _matmul_resid(a, w_all, layer, x, modtab, k_mod, blocks_per_batch, tm=1024, tn=512):
    m, k = a.shape
    n = w_all.shape[2]
    nrow = modtab.shape[0]
    return pl.pallas_call(
        functools.partial(_mm_resid_kernel, blocks_per_batch=blocks_per_batch),
        grid=(m // tm, n // tn),
        in_specs=[pl.BlockSpec((tm, k), lambda i, j: (i, 0)),
                  pl.BlockSpec((1, k, tn), lambda i, j: (layer, 0, j)),
                  pl.BlockSpec((tm, tn), lambda i, j: (i, j)),
                  pl.BlockSpec((nrow, 1, tn), lambda i, j: (0, 0, k_mod * (n // tn) + j))],
        out_specs=pl.BlockSpec((tm, tn), lambda i, j: (i, j)),
        out_shape=jax.ShapeDtypeStruct((m, n), F32),
        compiler_params=_cp(("parallel", "parallel")),
        name="matmul_resid",
    )(a, w_all, x, modtab)


def _merge_kernel(ya_ref, yb_ref, yc_ref, yd_ref, ga_ref, gb_ref, gc_ref, gd_ref, wb_ref, o_ref):
    acc = None
    for i, (y_ref, g_ref) in enumerate(((ya_ref, ga_ref), (yb_ref, gb_ref),
                                        (yc_ref, gc_ref), (yd_ref, gd_ref))):
        term = _sigmoid(g_ref[...].astype(F32)) * _dot(y_ref[...], wb_ref[0, i])
        acc = term if acc is None else acc + term
    o_ref[...] = acc.astype(o_ref.dtype)


def _merge(ys, gates, wb_all, layer, tm=1024, tn=512):
    m, bw = ys[0].shape
    n = wb_all.shape[3]
    nj = n // tn
    y_spec = pl.BlockSpec((tm, bw), lambda i, j: (i, 0))
    g_specs = [pl.BlockSpec((tm, tn), functools.partial(lambda i, j, q: (i, q * nj + j), q=q))
               for q in range(N_BRANCH)]
    return pl.pallas_call(
        _merge_kernel,
        grid=(m // tm, nj),
        in_specs=[y_spec] * N_BRANCH + g_specs + [
            pl.BlockSpec((1, N_BRANCH, bw, tn), lambda i, j: (layer, 0, 0, j))],
        out_specs=pl.BlockSpec((tm, tn), lambda i, j: (i, j)),
        out_shape=jax.ShapeDtypeStruct((m, n), BF16),
        compiler_params=_cp(("parallel", "parallel")),
        name="branch_merge",
    )(*ys, gates, gates, gates, gates, wb_all)


def _s5_pack_kernel(x_ref, o_ref, xf, ob):
    nb = x_ref.shape[0]
    cpb = S5_PACK_ROWS // S5_CHUNK
    ppt = LANES // S5_PIECE
    for j in range(BRANCH_W // LANES):
        xf[j] = x_ref[:, :, j * LANES:(j + 1) * LANES].astype(F32)
    for t in range(S5_CHUNK):
        for j in range(BRANCH_W // LANES):
            rows = xf[j, :, pl.ds(t, cpb, stride=S5_CHUNK), :].reshape(nb * cpb, LANES)
            for k in range(ppt):
                ob[j * ppt + k, :, t * S5_PIECE:(t + 1) * S5_PIECE] = rows[:, k * S5_PIECE:(k + 1) * S5_PIECE]
    o_ref[...] = ob[...].astype(o_ref.dtype)


def _s5_pack(p1_3d):
    nb, tlen, _ = p1_3d.shape
    rows = nb * (S5_PACK_ROWS // S5_CHUNK)
    return pl.pallas_call(
        _s5_pack_kernel,
        grid=(tlen // S5_PACK_ROWS,),
        in_specs=[pl.BlockSpec((nb, S5_PACK_ROWS, BRANCH_W), lambda i: (0, i, 0))],
        out_specs=pl.BlockSpec((S5_NPAIR, rows, BRANCH_W), lambda i: (0, i, 0)),
        out_shape=jax.ShapeDtypeStruct((S5_NPAIR, (tlen // S5_PACK_ROWS) * rows, BRANCH_W), BF16),
        scratch_shapes=[pltpu.VMEM((BRANCH_W // LANES, nb, S5_PACK_ROWS, LANES), F32),
                        pltpu.VMEM((S5_NPAIR, rows, BRANCH_W), F32)],
        compiler_params=_cp(("parallel",)),
        name="s5_pack",
    )(p1_3d)


def _s5_kernel(u_ref, ball_ref, toep_ref, call_ref, lam_ref, y_ref, st, *, nb, ctx_chunks):
    u = u_ref[0]
    s_all = _dot(u, ball_ref[0])
    for k in range(4):
        st[k] = s_all[:, k * LANES:(k + 1) * LANES]
    cpb = S5_PACK_ROWS // S5_CHUNK
    nchunk = u.shape[0] // nb
    lam = lam_ref[0]
    lfr, lfi, lbr, lbi = [jnp.broadcast_to(lam[:, k * LANES:(k + 1) * LANES], (nb, LANES)) for k in range(4)]

    def chunk_rows(c):
        return pl.ds((c // cpb) * (cpb * nb) + c % cpb, nb, stride=cpb)

    def step(rows, k, xr, xi, lr, li):
        s_r = st[2 * k, rows, :]
        s_i = st[2 * k + 1, rows, :]
        st[2 * k, rows, :] = xr
        st[2 * k + 1, rows, :] = xi
        return lr * xr - li * xi + s_r, lr * xi + li * xr + s_i

    def body(k, carry):
        fr, fi, br, bi = carry
        fr, fi = step(chunk_rows(k), 0, fr, fi, lfr, lfi)
        cb = jnp.where(k < ctx_chunks, ctx_chunks - 1 - k, nchunk - 1 - (k - ctx_chunks))
        br, bi = step(chunk_rows(cb), 1, br, bi, lbr, lbi)
        return fr, fi, br, bi

    zero = jnp.zeros((nb, LANES), F32)
    lax.fori_loop(0, nchunk, body, (zero, zero, zero, zero))
    states = jnp.concatenate([st[k] for k in range(4)], axis=-1).astype(BF16)
    y_ref[0] = _dot(u, toep_ref[0]) + _dot(states, call_ref[0])


def _s5_params(a_re, a_im, log_step, b_re, b_im, c_re, c_im):
    L, P, CH, NP = S5_CHUNK, S5_STATE, S5_CH, S5_NPAIR
    lam = lax.complex(a_re, a_im)
    lam_dt = lam * jnp.exp(log_step)[..., None]
    lam_bar = jnp.exp(lam_dt)
    b_bar = ((lam_bar - 1.0) / lam)[..., None] * lax.complex(b_re, b_im)
    c_mat = lax.complex(c_re, c_im)
    tau = jnp.arange(L + 1, dtype=F32).astype(jnp.complex64)
    pw = jnp.exp(lam_dt[:, None] * tau[None, :, None, None])
    hi = lax.Precision.HIGHEST
    kern = jnp.einsum('zgip,ztgp,zgpj->zgtij', c_mat, pw[:, :L], b_bar, precision=hi).real
    kern = jnp.concatenate([kern, jnp.zeros_like(kern[:, :, :1])], axis=2)
    s_idx = jnp.arange(L)[:, None]
    t_idx = jnp.arange(L)[None, :]
    sel_f = jnp.where(t_idx >= s_idx, t_idx - s_idx, L)
    sel_b = jnp.where(s_idx >= t_idx, s_idx - t_idx, L)
    toep = kern[0][:, sel_f] + kern[1][:, sel_b]
    eye2 = jnp.eye(2, dtype=F32)
    toep = jnp.einsum('qgstij,gh->qsgjthi', toep.reshape(NP, 2, L, L, CH, CH), eye2)
    toep = toep.reshape(NP, L * S5_PIECE, L * S5_PIECE)

    bend_f = pw[0, :L][::-1][:, :, None, :] * jnp.swapaxes(b_bar[0], 1, 2)[None]
    bend_b = pw[1, :L][:, :, None, :] * jnp.swapaxes(b_bar[1], 1, 2)[None]
    parts = jnp.stack([bend_f.real, bend_f.imag, bend_b.real, bend_b.imag])
    parts = parts.reshape(4, L, NP, 2, CH, P)
    ball = jnp.einsum('asqgjp,gh->qsgjahp', parts, eye2).reshape(NP, L * S5_PIECE, 8 * P)

    cout_f = c_mat[0][None] * pw[0, 1:][:, :, None, :]
    cout_b = c_mat[1][None] * pw[1, 1:][::-1][:, :, None, :]
    parts = jnp.stack([cout_f.real, -cout_f.imag, cout_b.real, -cout_b.imag])
    parts = parts.reshape(4, L, NP, 2, CH, P)
    call = jnp.einsum('atqgip,gh->qagpthi', parts, eye2).reshape(NP, 8 * P, L * S5_PIECE)

    lam_l = jnp.stack([pw[0, L].real, pw[0, L].imag, pw[1, L].real, pw[1, L].imag])
    lam_l = jnp.transpose(lam_l.reshape(4, NP, 2 * P), (1, 0, 2)).reshape(NP, 1, 8 * P)
    return ball.astype(BF16), toep.astype(BF16), call.astype(BF16), lam_l


def _s5_scan(lhs, prm, nb):
    npair, rows, w = lhs.shape
    ball, toep, call, lam_l = prm
    spec_w = pl.BlockSpec((1, w, w), lambda q: (q, 0, 0))
    return pl.pallas_call(
        functools.partial(_s5_kernel, nb=nb, ctx_chunks=CTX_LEN // S5_CHUNK),
        grid=(npair,),
        in_specs=[pl.BlockSpec((1, rows, w), lambda q: (q, 0, 0)), spec_w, spec_w, spec_w,
                  pl.BlockSpec((1, 1, w), lambda q: (q, 0, 0))],
        out_specs=pl.BlockSpec((1, rows, w), lambda q: (q, 0, 0)),
        out_shape=jax.ShapeDtypeStruct((npair, rows, w), F32),
        scratch_shapes=[pltpu.VMEM((4, rows, LANES), F32)],
        compiler_params=_cp(("parallel",)),
        name="s5_scan",
    )(lhs, ball, toep, call, lam_l)


def _glu_kernel(y_ref, u_ref, d_ref, w_ref, b_ref, o_ref, yn):
    nb = u_ref.shape[0]
    cpb = S5_PACK_ROWS // S5_CHUNK
    ppt = LANES // S5_PIECE
    ntile = BRANCH_W // LANES
    for t in range(S5_CHUNK):
        for j in range(ntile):
            tile = jnp.concatenate([y_ref[j * ppt + k, :, t * S5_PIECE:(t + 1) * S5_PIECE]
                                    for k in range(ppt)], axis=-1)
            yn[j, :, pl.ds(t, cpb, stride=S5_CHUNK), :] = tile.reshape(nb, cpb, LANES)
    yd = jnp.concatenate([yn[j] for j in range(ntile)], axis=-1)
    y = d_ref[...] * u_ref[...].astype(F32).reshape(nb * S5_PACK_ROWS, BRANCH_W)
    y = y + yd.reshape(nb * S5_PACK_ROWS, BRANCH_W)
    cdf = 0.5 * (1.0 + jnp.tanh(math.sqrt(2.0 / math.pi) * (y + 0.044715 * (y * y * y))))
    g = y * cdf
    z = _dot(g.astype(BF16), w_ref[0]) + b_ref[...]
    o_ref[...] = (g * _sigmoid(z)).astype(o_ref.dtype).reshape(o_ref.shape)


def _s5_glu(y, p1_3d, d_skip, w_glu_all, layer, b_glu):
    nb, tlen, _ = p1_3d.shape
    bw = BRANCH_W
    rows = nb * (S5_PACK_ROWS // S5_CHUNK)
    const = lambda i: (0, 0)
    blk = pl.BlockSpec((nb, S5_PACK_ROWS, bw), lambda i: (0, i, 0))
    return pl.pallas_call(
        _glu_kernel,
        grid=(tlen // S5_PACK_ROWS,),
        in_specs=[pl.BlockSpec((S5_NPAIR, rows, bw), lambda i: (0, i, 0)), blk,
                  pl.BlockSpec((1, bw), const), pl.BlockSpec((1, bw, bw), lambda i: (layer, 0, 0)),
                  pl.BlockSpec((1, bw), const)],
        out_specs=blk,
        out_shape=jax.ShapeDtypeStruct((nb, tlen, bw), BF16),
        scratch_shapes=[pltpu.VMEM((bw // LANES, nb, S5_PACK_ROWS, LANES), F32)],
        compiler_params=_cp(("parallel",)),
        name="s5_glu",
    )(y, p1_3d, d_skip.reshape(1, bw), w_glu_all, b_glu.reshape(1, bw))


def _lane_masks(rows):
    lane = lax.broadcasted_iota(jnp.int32, (rows, LANES), 1)
    return lane < HEAD_DIM, (lane % HEAD_DIM) < (HEAD_DIM // 2)


def _head_norm(x, gain, lo):
    ss = x * x
    s_lo = jnp.sum(jnp.where(lo, ss, 0.0), axis=-1, keepdims=True)
    s_hi = jnp.sum(jnp.where(lo, 0.0, ss), axis=-1, keepdims=True)
    ms = jnp.where(lo, s_lo, s_hi) * (1.0 / HEAD_DIM)
    return x * lax.rsqrt(ms + EPS) * gain


def _rope(x, cos, sin_signed, first_half):
    partner = jnp.where(first_half, pltpu.roll(x, LANES - HEAD_DIM // 2, 1),
                        pltpu.roll(x, HEAD_DIM // 2, 1))
    return x * cos + partner * sin_signed


def _prep_slab(x, gain, cos, sin_signed, scale):
    rows = x.shape[0]
    lo, first_half = _lane_masks(rows)
    y = _head_norm(x.astype(F32), gain, lo)
    if cos is not None:
        y = _rope(y, cos, sin_signed, first_half)
    if scale is not None:
        y = y * scale
    return y.astype(BF16)


def _prep_keys(k_ref, kn_ref, gain, cos_ref, sin_ref, nslab):
    tlen = k_ref.shape[1]
    for r0 in range(0, tlen, SEG_ROWS):
        rows = slice(r0, r0 + SEG_ROWS)
        for s in range(nslab):
            cols = slice(s * LANES, (s + 1) * LANES)
            cos = None if cos_ref is None else cos_ref[rows, :]
            sin = None if sin_ref is None else sin_ref[rows, :]
            kn_ref[rows, cols] = _prep_slab(k_ref[0, rows, cols], gain, cos, sin, None)


def _prep_values(v_ref, va_ref, nslab):
    tlen = v_ref.shape[1]
    for r0 in range(0, tlen, SEG_ROWS):
        rows = slice(r0, r0 + SEG_ROWS)
        lo, _ = _lane_masks(SEG_ROWS)
        for s in range(nslab):
            cols = slice(s * LANES, (s + 1) * LANES)
            v = v_ref[0, rows, cols]
            one = jnp.ones_like(v)
            va_ref[0, rows, cols] = jnp.where(lo, v, one)
            va_ref[1, rows, cols] = jnp.where(lo, one, v)


def _split_heads(q):
    lo, _ = _lane_masks(q.shape[0])
    zero = jnp.zeros_like(q)
    return jnp.where(lo, q, zero), jnp.where(lo, zero, q), lo


def _join_heads(o_lo, o_hi, extra_lo, extra_hi, lo):
    num = jnp.where(lo, o_lo, o_hi)
    den = pltpu.roll(jnp.where(lo, o_hi, o_lo), HEAD_DIM, 1)
    if extra_lo is not None:
        den = den + jnp.where(lo, extra_lo, extra_hi)
    return num / den


def _win_kernel(sink_ref, q_ref, k_ref, v_ref, cos_ref, sin_ref, qg_ref, kg_ref, o_ref,
                kn_ref, va_ref, qs_ref, e_ref):
    qi = pl.program_id(1)
    tq = q_ref.shape[1]
    tlen = k_ref.shape[1]
    span = tq + 2 * WINDOW
    nhead = WIN_HEADS

    @pl.when(qi == 0)
    def _():
        _prep_keys(k_ref, kn_ref, kg_ref[...], cos_ref, sin_ref, 1)
        _prep_values(v_ref, va_ref, 1)

    qrows = pl.ds(pl.multiple_of(qi * tq, tq), tq)
    cos_q = cos_ref[qrows, :]
    sin_q = sin_ref[qrows, :]

    def attend(local):
        for s in range(nhead // 2):
            cols = slice(s * LANES, (s + 1) * LANES)
            qn = _prep_slab(q_ref[0, :, cols], qg_ref[...], cos_q, sin_q, SCALE * LOG2E)
            q_lo, q_hi, _ = _split_heads(qn)
            qs_ref[(2 * s) * tq:(2 * s + 1) * tq, :] = q_lo
            qs_ref[(2 * s + 1) * tq:(2 * s + 2) * tq, :] = q_hi
        qs = qs_ref[...]
        s_ctx = _dot_t(qs, kn_ref[0:CTX_LEN, :])
        m = jnp.max(s_ctx, axis=-1, keepdims=True)
        if local:
            start = jnp.clip(qi * tq - WINDOW, CTX_LEN, tlen - span)
            krows = pl.ds(pl.multiple_of(start, LANES), span)
            row = lax.broadcasted_iota(jnp.int32, (nhead * tq, span), 0)
            qtok = qi * tq + row % tq
            ktok = start + lax.broadcasted_iota(jnp.int32, (nhead * tq, span), 1)
            s_loc = jnp.where(jnp.abs(qtok - ktok) <= WINDOW, _dot_t(qs, kn_ref[krows, :]), NEG_INF)
            m = jnp.maximum(m, jnp.max(s_loc, axis=-1, keepdims=True))
        sinks = [sink_ref[(hb // 2) + (hb % 2) * (nhead // 2)] * LOG2E for hb in range(nhead)]
        m = jnp.concatenate([jnp.maximum(m[hb * tq:(hb + 1) * tq], sinks[hb]) for hb in range(nhead)], axis=0)
        e_ref[:, 0:CTX_LEN] = jnp.exp2(s_ctx - m).astype(BF16)
        if local:
            e_ref[:, CTX_LEN:CTX_LEN + span] = jnp.exp2(s_loc - m).astype(BF16)
        lo, _ = _lane_masks(tq)
        for s in range(nhead // 2):
            outs, extra = [], []
            for half in range(2):
                hb = 2 * s + half
                rows = slice(hb * tq, (hb + 1) * tq)
                o = _dot(e_ref[rows, 0:CTX_LEN], va_ref[half, 0:CTX_LEN, :])
                if local:
                    o = o + _dot(e_ref[rows, CTX_LEN:CTX_LEN + span], va_ref[half, krows, :])
                outs.append(o)
                extra.append(jnp.exp2(sinks[hb] - m[rows]))
            o_ref[0, :, s * LANES:(s + 1) * LANES] = _join_heads(
                outs[0], outs[1], extra[0], extra[1], lo).astype(o_ref.dtype)

    @pl.when(qi < CTX_LEN // tq)
    def _():
        attend(False)

    @pl.when(qi >= CTX_LEN // tq)
    def _():
        attend(True)


def _win_attn(p1, sink, cos_t, sin_t, qg, kg, tq=128):
    bsz, tlen, _ = p1.shape
    kcol = 8 * BRANCH_W // LANES
    return pl.pallas_call(
        _win_kernel,
        grid=(bsz, tlen // tq),
        in_specs=[pl.BlockSpec(memory_space=pltpu.SMEM),
                  pl.BlockSpec((1, tq, BRANCH_W), lambda b, i: (b, i, 1)),
                  pl.BlockSpec((1, tlen, LANES), lambda b, i: (b, 0, kcol)),
                  pl.BlockSpec((1, tlen, LANES), lambda b, i: (b, 0, kcol + 1)),
                  pl.BlockSpec((tlen, LANES), lambda b, i: (0, 0)),
                  pl.BlockSpec((tlen, LANES), lambda b, i: (0, 0)),
                  pl.BlockSpec((1, LANES), lambda b, i: (0, 0)),
                  pl.BlockSpec((1, LANES), lambda b, i: (0, 0))],
        out_specs=pl.BlockSpec((1, tq, BRANCH_W), lambda b, i: (b, i, 0)),
        out_shape=jax.ShapeDtypeStruct((bsz, tlen, BRANCH_W), BF16),
        scratch_shapes=[pltpu.VMEM((tlen, LANES), BF16), pltpu.VMEM((2, tlen, LANES), BF16),
                        pltpu.VMEM((WIN_HEADS * tq, LANES), BF16),
                        pltpu.VMEM((WIN_HEADS * tq, CTX_LEN + tq + 2 * WINDOW), BF16)],
        compiler_params=_cp(("parallel", "arbitrary")),
        name="win_attn",
    )(sink, p1, p1, p1, cos_t, sin_t, qg, kg)


def _diff_kernel(lam_ref, q_ref, k_ref, v_ref, cos_ref, sin_ref, qg_ref, kg_ref, sg_ref, o_ref, kn_ref):
    qi = pl.program_id(1)
    tq = q_ref.shape[1]
    tlen = k_ref.shape[1]
    lam = lam_ref[0]
    out_scale = lam_ref[1]

    @pl.when(qi == 0)
    def _():
        _prep_keys(k_ref, kn_ref, kg_ref[...], cos_ref, sin_ref, DIFF_HEADS)

    qrows = pl.ds(pl.multiple_of(qi * tq, tq), tq)
    cos_q = cos_ref[qrows, :]
    sin_q = sin_ref[qrows, :]

    def attend(nk):
        for h in range(DIFF_HEADS):
            cols = slice(h * LANES, (h + 1) * LANES)
            qn = _prep_slab(q_ref[0, :, cols], qg_ref[...], cos_q, sin_q, SCALE * LOG2E)
            q1, q2, _ = _split_heads(qn)
            keys = kn_ref[0:nk, cols]
            vals = v_ref[0, 0:nk, cols]
            s1 = _dot_t(q1, keys)
            s2 = _dot_t(q2, keys)
            e1 = jnp.exp2(s1 - jnp.max(s1, axis=-1, keepdims=True))
            e2 = jnp.exp2(s2 - jnp.max(s2, axis=-1, keepdims=True))
            r1 = 1.0 / _rowsum(e1)
            r2 = lam / _rowsum(e2)
            o = _dot(e1.astype(BF16), vals) * r1 - _dot(e2.astype(BF16), vals) * r2
            o = o * lax.rsqrt(jnp.mean(o * o, axis=-1, keepdims=True) + EPS) * sg_ref[...]
            o_ref[0, :, cols] = (o * out_scale).astype(o_ref.dtype)

    @pl.when(qi < CTX_LEN // tq)
    def _():
        attend(CTX_LEN)

    @pl.when(qi >= CTX_LEN // tq)
    def _():
        attend(tlen)


def _diff_attn(p1, lam_vec, cos_t, sin_t, qg, kg, sg, tq=256):
    bsz, tlen, _ = p1.shape
    full = lambda c: pl.BlockSpec((1, tlen, BRANCH_W), functools.partial(lambda b, i, c: (b, 0, c), c=c))
    const = lambda b, i: (0, 0)
    return pl.pallas_call(
        _diff_kernel,
        grid=(bsz, tlen // tq),
        in_specs=[pl.BlockSpec(memory_space=pltpu.SMEM),
                  pl.BlockSpec((1, tq, BRANCH_W), lambda b, i: (b, i, 2)),
                  full(3), full(4),
                  pl.BlockSpec((tlen, LANES), const), pl.BlockSpec((tlen, LANES), const),
                  pl.BlockSpec((1, LANES), const), pl.BlockSpec((1, LANES), const),
                  pl.BlockSpec((1, LANES), const)],
        out_specs=pl.BlockSpec((1, tq, BRANCH_W), lambda b, i: (b, i, 0)),
        out_shape=jax.ShapeDtypeStruct((bsz, tlen, BRANCH_W), BF16),
        scratch_shapes=[pltpu.VMEM((tlen, BRANCH_W), BF16)],
        compiler_params=_cp(("parallel", "arbitrary")),
        name="diff_attn",
    )(lam_vec, p1, p1, p1, cos_t, sin_t, qg, kg, sg)


def _na_kernel(q_ref, k_ref, v_ref, bias_ref, qg_ref, kg_ref, o_ref, kn_ref, va_ref, qs_ref, sc_ref, e_ref,
               *, grid_rows):
    qi = pl.program_id(1)
    span = NA_WIN_R * GRID_W
    nslab = NA_HEADS // 2

    @pl.when(qi == 0)
    def _():
        _prep_keys(k_ref, kn_ref, kg_ref[...], None, None, nslab)
        _prep_values(v_ref, va_ref, nslab)

    def attend(qrows, nq, krows):
        for s in range(nslab):
            cols = slice(s * LANES, (s + 1) * LANES)
            qn = _prep_slab(q_ref[0, qrows, cols], qg_ref[...], None, None, SCALE * LOG2E)
            q_lo, q_hi, _ = _split_heads(qn)
            qs_ref[(2 * s) * nq:(2 * s + 1) * nq, :] = q_lo
            qs_ref[(2 * s + 1) * nq:(2 * s + 2) * nq, :] = q_hi
        ncol = CTX_LEN if krows is None else CTX_LEN + span
        for s in range(nslab):
            cols = slice(s * LANES, (s + 1) * LANES)
            rows = slice(2 * s * nq, (2 * s + 2) * nq)
            sc_ref[rows, 0:CTX_LEN] = _dot_t(qs_ref[rows, :], kn_ref[0:CTX_LEN, cols])
            if krows is not None:
                bias = bias_ref[0, 2 * s:2 * s + 2].reshape(2 * nq, span)
                sc_ref[rows, CTX_LEN:ncol] = _dot_t(qs_ref[rows, :], kn_ref[krows, cols]) + bias
        nrow = NA_HEADS * nq
        sc = sc_ref[0:nrow, 0:ncol]
        e_ref[0:nrow, 0:ncol] = jnp.exp2(sc - jnp.max(sc, axis=-1, keepdims=True)).astype(BF16)
        lo, _ = _lane_masks(nq)
        for s in range(nslab):
            cols = slice(s * LANES, (s + 1) * LANES)
            outs = []
            for half in range(2):
                rows = slice((2 * s + half) * nq, (2 * s + half + 1) * nq)
                o = _dot(e_ref[rows, 0:CTX_LEN], va_ref[half, 0:CTX_LEN, cols])
                if krows is not None:
                    o = o + _dot(e_ref[rows, CTX_LEN:ncol], va_ref[half, krows, cols])
                outs.append(o)
            o_ref[0, qrows, cols] = _join_heads(outs[0], outs[1], None, None, lo).astype(o_ref.dtype)

    @pl.when(qi == 0)
    def _():
        attend(slice(0, CTX_LEN), CTX_LEN, None)

    @pl.when(qi > 0)
    def _():
        r = qi - 1
        qrows = pl.ds(pl.multiple_of(CTX_LEN + r * GRID_W, GRID_W), GRID_W)
        k0 = jnp.clip(r - NA_WIN_R // 2, 0, grid_rows - NA_WIN_R)
        krows = pl.ds(pl.multiple_of(CTX_LEN + k0 * GRID_W, GRID_W), span)
        attend(qrows, GRID_W, krows)


def _na_bias(rpb_all, rows):
    win_r = NA_WIN_R
    half = win_r // 2
    r = jnp.concatenate([jnp.arange(half + 1), jnp.arange(rows - half + 1, rows)])
    row_idx = jnp.clip(r - half, 0, rows - win_r)[:, None] + jnp.arange(win_r)[None, :]
    r_off = row_idx - r[:, None] + (NA_WIN_R - 1)
    col = jnp.arange(GRID_W)
    c_off = jnp.clip(col[None, :] - col[:, None] + (NA_WIN_C - 1), 0, 2 * NA_WIN_C - 2)
    onehot = (c_off[None] == jnp.arange(2 * NA_WIN_C - 1)[:, None, None]).astype(F32)
    rows_sel = rpb_all.astype(F32)[:, :, r_off] * LOG2E
    bias = jnp.einsum('lhpwk,kqc->lphqwc', rows_sel, onehot, precision=lax.Precision.HIGHEST)
    col_start = jnp.clip(col - NA_WIN_C // 2, 0, GRID_W - NA_WIN_C)
    col_ok = (col[None, :] >= col_start[:, None]) & (col[None, :] < col_start[:, None] + NA_WIN_C)
    bias = jnp.where(col_ok[None, None, None, :, None, :], bias, NEG_INF)
    depth = rpb_all.shape[0]
    return bias.reshape(depth, r.shape[0], NA_HEADS, GRID_W, win_r * GRID_W)


def _na_attn(p1, bias_all, layer, qg, kg):
    bsz, tlen, _ = p1.shape
    rows = (tlen - CTX_LEN) // GRID_W
    half = NA_WIN_R // 2
    full = lambda c: pl.BlockSpec((1, tlen, BRANCH_W), functools.partial(lambda b, i, c: (b, 0, c), c=c))
    const = lambda b, i: (0, 0)

    def bias_idx(b, i):
        r = jnp.maximum(i - 1, 0)
        return (layer, jnp.minimum(r, half) + jnp.maximum(r - (rows - half), 0), 0, 0, 0)

    return pl.pallas_call(
        functools.partial(_na_kernel_wrap, grid_rows=rows),
        grid=(bsz, rows + 1),
        in_specs=[full(5), full(6), full(7),
                  pl.BlockSpec((1, 1, NA_HEADS, GRID_W, NA_WIN_R * GRID_W), bias_idx),
                  pl.BlockSpec((1, LANES), const), pl.BlockSpec((1, LANES), const)],
        out_specs=pl.BlockSpec((1, tlen, BRANCH_W), lambda b, i: (b, 0, 0)),
        out_shape=jax.ShapeDtypeStruct((bsz, tlen, BRANCH_W), BF16),
        scratch_shapes=[pltpu.VMEM((tlen, BRANCH_W), BF16), pltpu.VMEM((2, tlen, BRANCH_W), BF16),
                        pltpu.VMEM((NA_HEADS * CTX_LEN, LANES), BF16),
                        pltpu.VMEM((NA_HEADS * CTX_LEN, CTX_LEN + NA_WIN_R * GRID_W), F32),
                        pltpu.VMEM((NA_HEADS * CTX_LEN, CTX_LEN + NA_WIN_R * GRID_W), BF16)],
        compiler_params=_cp(("parallel", "arbitrary")),
        name="na_attn",
    )(p1, p1, p1, bias_all, qg, kg)


def _na_kernel_wrap(q_ref, k_ref, v_ref, bias_ref, *rest, grid_rows):
    _na_kernel(q_ref, k_ref, v_ref, bias_ref.at[0], *rest, grid_rows=grid_rows)


def _row_copy(src_hbm, row, dst, r, sem):
    return pltpu.make_async_copy(src_hbm.at[pl.ds(row, 1), :], dst.at[pl.ds(r, 1), :], sem)


def _gather_rows(idx_ref, base, src_hbm, dst, sem, nrows):
    for r in range(nrows):
        _row_copy(src_hbm, idx_ref[base + r], dst, r, sem).start()


def _gather_wait(src_hbm, dst, sem, nrows):
    pltpu.make_async_copy(src_hbm.at[pl.ds(0, nrows), :], dst, sem).wait()


def _expert_kernel(te_ref, rt_ref, na_ref, h_hbm, rw_ref, w1_ref, w3_ref, w2_ref, o_ref, buf, sem):
    i = pl.program_id(0)
    tm = buf.shape[1]
    slot = i % 2
    nact = jnp.maximum(na_ref[0], 1)

    @pl.when(i == 0)
    def _():
        _gather_rows(rt_ref, 0, h_hbm, buf.at[0], sem.at[0], tm)

    @pl.when(i < nact)
    def _():
        _gather_wait(h_hbm, buf.at[slot], sem.at[slot], tm)
        nxt = jnp.minimum(i + 1, pl.num_programs(0) - 1)
        _gather_rows(rt_ref, nxt * tm, h_hbm, buf.at[1 - slot], sem.at[1 - slot], tm)
        x = buf[slot].astype(BF16)
        a = _dot(x, w1_ref[0, 0])
        mid = (a * jax.nn.sigmoid(a)) * _dot(x, w3_ref[0, 0])
        o_ref[...] = _dot(mid.astype(BF16), w2_ref[0, 0]) * rw_ref[...]

    @pl.when(i == nact - 1)
    def _():
        _gather_wait(h_hbm, buf.at[1 - slot], sem.at[1 - slot], tm)

    @pl.when(i >= nact)
    def _():
        o_ref[...] = jnp.zeros_like(o_ref)


def _moe_experts(h, tile_expert, row_token, n_active, row_weight, w1_all, w3_all, w2_all, layer, tm=MOE_TM):
    n, d = h.shape
    rmax = row_token.shape[0]
    ff = w1_all.shape[3]
    grid_spec = pltpu.PrefetchScalarGridSpec(
        num_scalar_prefetch=3,
        grid=(rmax // tm,),
        in_specs=[pl.BlockSpec(memory_space=pl.ANY),
                  pl.BlockSpec((tm, 1), lambda i, te, rt, na: (i, 0)),
                  pl.BlockSpec((1, 1, d, ff), lambda i, te, rt, na: (layer, te[i], 0, 0)),
                  pl.BlockSpec((1, 1, d, ff), lambda i, te, rt, na: (layer, te[i], 0, 0)),
                  pl.BlockSpec((1, 1, ff, d), lambda i, te, rt, na: (layer, te[i], 0, 0))],
        out_specs=pl.BlockSpec((tm, d), lambda i, te, rt, na: (i, 0)),
        scratch_shapes=[pltpu.VMEM((2, tm, d), F32), pltpu.SemaphoreType.DMA((2,))])
    return pl.pallas_call(
        _expert_kernel,
        grid_spec=grid_spec,
        out_shape=jax.ShapeDtypeStruct((rmax, d), F32),
        compiler_params=_cp(("arbitrary",)),
        name="moe_experts",
    )(tile_expert, row_token, n_active, h, row_weight, w1_all, w3_all, w2_all)


def _combine_kernel(d0_ref, d1_ref, ys_hbm, x_ref, mod_ref, o_ref, buf0, buf1, sem,
                    *, blocks_per_batch):
    i = pl.program_id(0)
    n = pl.num_programs(0)
    tm = x_ref.shape[0]
    slot = i % 2

    def issue(tile, s):
        _gather_rows(d0_ref, tile * tm, ys_hbm, buf0.at[s], sem.at[0, s], tm)
        _gather_rows(d1_ref, tile * tm, ys_hbm, buf1.at[s], sem.at[1, s], tm)

    @pl.when(i == 0)
    def _():
        issue(0, 0)

    @pl.when(i + 1 < n)
    def _():
        issue(i + 1, 1 - slot)

    _gather_wait(ys_hbm, buf0.at[slot], sem.at[0, slot], tm)
    _gather_wait(ys_hbm, buf1.at[slot], sem.at[1, slot], tm)
    row = _seg_mod_row(i, blocks_per_batch)
    o_ref[...] = x_ref[...] + mod_ref[row] * (buf0[slot] + buf1[slot])


def _moe_combine(ys, dest0, dest1, x, modtab, k_mod, blocks_per_batch):
    n, d = x.shape
    tm = SEG_ROWS
    nrow = modtab.shape[0]
    grid_spec = pltpu.PrefetchScalarGridSpec(
        num_scalar_prefetch=2,
        grid=(n // tm,),
        in_specs=[pl.BlockSpec(memory_space=pl.ANY),
                  pl.BlockSpec((tm, d), lambda i, a, b: (i, 0)),
                  pl.BlockSpec((nrow, 1, d), lambda i, a, b: (0, 0, k_mod))],
        out_specs=pl.BlockSpec((tm, d), lambda i, a, b: (i, 0)),
        scratch_shapes=[pltpu.VMEM((2, tm, d), F32), pltpu.VMEM((2, tm, d), F32),
                        pltpu.SemaphoreType.DMA((2, 2))])
    return pl.pallas_call(
        functools.partial(_combine_kernel, blocks_per_batch=blocks_per_batch),
        grid_spec=grid_spec,
        out_shape=jax.ShapeDtypeStruct((n, d), F32),
        compiler_params=_cp(("arbitrary",)),
        name="moe_combine",
    )(dest0, dest1, ys, x, modtab)


def _route(logits, tm=MOE_TM):
    n = logits.shape[0]
    g_prob = jax.nn.softmax(logits[:, :N_GROUPS], axis=-1)
    g_idx = jnp.argmax(g_prob, axis=-1).astype(jnp.int32)
    g_w = jnp.max(g_prob, axis=-1)
    e_logits = logits[:, N_GROUPS:N_GROUPS + N_EXPERTS].reshape(n, N_GROUPS, EXPERTS_PER_GROUP)
    in_group = jnp.arange(N_GROUPS, dtype=jnp.int32)[None, :, None] == g_idx[:, None, None]
    e_in = jnp.sum(jnp.where(in_group, e_logits, 0.0), axis=1)
    slot = jnp.arange(EXPERTS_PER_GROUP, dtype=jnp.int32)[None, :]
    i1 = jnp.argmax(e_in, axis=-1).astype(jnp.int32)
    v1 = jnp.max(e_in, axis=-1)
    rest = jnp.where(slot == i1[:, None], -jnp.inf, e_in)
    i2 = jnp.argmax(rest, axis=-1).astype(jnp.int32)
    v2 = jnp.max(rest, axis=-1)
    w_sel = jax.nn.softmax(jnp.stack([v1, v2], axis=-1), axis=-1) * g_w[:, None]
    expert = g_idx[:, None] * EXPERTS_PER_GROUP + jnp.stack([i1, i2], axis=-1)

    e_flat = expert.reshape(-1)
    w_flat = w_sel.reshape(-1)
    ids = jnp.arange(N_EXPERTS, dtype=jnp.int32)
    onehot = (e_flat[:, None] == ids[None, :]).astype(jnp.int32)
    csum = jnp.cumsum(onehot, axis=0)
    rank = jnp.sum(onehot * csum, axis=1) - 1
    counts = csum[-1]
    padded = ((counts + tm - 1) // tm) * tm
    pend = jnp.cumsum(padded)
    pstart = pend - padded
    ustart = jnp.cumsum(counts) - counts
    dest = (jnp.sum(onehot * pstart[None, :], axis=1) + rank).astype(jnp.int32)

    rmax = 2 * n + N_EXPERTS * tm
    total = pend[-1]
    tile_start = jnp.arange(rmax // tm, dtype=jnp.int32) * tm
    tile_expert = jnp.minimum(jnp.sum((tile_start[:, None] >= pend[None, :]).astype(jnp.int32), axis=1),
                              N_EXPERTS - 1)
    order = jnp.argsort(e_flat, stable=True).astype(jnp.int32)
    r = jnp.arange(rmax, dtype=jnp.int32)
    e_r = tile_expert[r // tm]
    off = r - pstart[e_r]
    valid = (r < total) & (off < counts[e_r])
    src = order[jnp.clip(ustart[e_r] + off, 0, 2 * n - 1)]
    row_token = jnp.where(valid, src // 2, 0).astype(jnp.int32)
    row_weight = jnp.where(valid, w_flat[src], 0.0)
    n_active = (total // tm).astype(jnp.int32)
    last_expert = tile_expert[jnp.maximum(n_active - 1, 0)]
    tile_expert = jnp.where(tile_start < total, tile_expert, last_expert).astype(jnp.int32)
    dest2 = dest.reshape(n, 2)
    return (tile_expert, row_token, n_active.reshape(1), row_weight.reshape(rmax, 1),
            dest2[:, 0], dest2[:, 1])


def _rope_tables(n_lat):
    t = jnp.arange(n_lat, dtype=jnp.int32)
    row = (t // GRID_W).astype(F32)
    col = (t % GRID_W).astype(F32)
    per_axis = HEAD_DIM // 4
    inv_freq = ROPE_BASE ** (-jnp.arange(per_axis, dtype=F32) / per_axis)
    ang = jnp.concatenate([row[:, None] * inv_freq, col[:, None] * inv_freq], axis=-1)
    cos = jnp.concatenate([jnp.ones((CTX_LEN, HEAD_DIM // 2), F32), jnp.cos(ang)], axis=0)
    sin = jnp.concatenate([jnp.zeros((CTX_LEN, HEAD_DIM // 2), F32), jnp.sin(ang)], axis=0)
    cos_t = jnp.tile(cos, (1, 4))
    sin_t = jnp.tile(jnp.concatenate([-sin, sin], axis=-1), (1, 2))
    return cos_t, sin_t


def _win_head_perm():
    cols = []
    for s in range(WIN_HEADS // 2):
        for half in range(2):
            h = s + half * (WIN_HEADS // 2)
            cols.extend(range(h * HEAD_DIM, (h + 1) * HEAD_DIM))
    return cols


def _p1_weight(w_in_bf):
    bw = BRANCH_W
    kvw = WIN_KV_HEADS * HEAD_DIM
    hd = HEAD_DIM
    q0 = bw
    q_b = [w_in_bf[:, :, q0 + h * hd:q0 + (h + 1) * hd]
           for s in range(WIN_HEADS // 2) for h in (s, s + WIN_HEADS // 2)]
    kv0 = 2 * bw
    rest0 = kv0 + 2 * kvw
    pad = jnp.zeros(w_in_bf.shape[:2] + (bw - 2 * kvw,), w_in_bf.dtype)
    p1 = jnp.concatenate([w_in_bf[:, :, :bw]] + q_b + [w_in_bf[:, :, rest0:rest0 + 6 * bw],
                                                       w_in_bf[:, :, kv0:rest0], pad], axis=2)
    return p1, w_in_bf[:, :, rest0 + 6 * bw:]


def kernel(x, c, ctx, c_ctx, w_ada, b_ada, norm_mix, norm_ffn, w_in, s5_a_re, s5_a_im, s5_log_step, s5_b_re, s5_b_im, s5_c_re, s5_c_im, s5_d, s5_w_glu, s5_b_glu, win_qn, win_kn, win_sink, diff_qn, diff_kn, diff_lambda, diff_subln, na_qn, na_kn, na_rpb, w_branch, w_out, moe_w_group, moe_b_group, moe_w_expert, moe_b_expert, moe_w1, moe_w3, moe_w2):
    bsz, n_lat, d = x.shape
    tlen = CTX_LEN + n_lat
    ntok = bsz * tlen
    bpb = tlen // SEG_ROWS
    depth = w_ada.shape[0]

    cond = jnp.zeros((ADA_ROWS, d), F32).at[:bsz].set(c).at[bsz].set(c_ctx)
    mod = _ada_mod(cond, w_ada, b_ada)
    mod_ctx = jnp.broadcast_to(mod[:, bsz:bsz + 1], (depth, bsz, 6 * d))
    modtab = jnp.stack([mod_ctx, mod[:, :bsz]], axis=2).reshape(depth, 2 * bsz, 1, 6 * d)

    w_p1, w_gate = _p1_weight(w_in.astype(BF16))
    wb_all = w_branch.astype(BF16)
    wb_win = jnp.concatenate([wb_all[:, 1:2, h * HEAD_DIM:(h + 1) * HEAD_DIM]
                              for s in range(WIN_HEADS // 2) for h in (s, s + WIN_HEADS // 2)], axis=2)
    wb_all = jnp.concatenate([wb_all[:, :1], wb_win, wb_all[:, 2:]], axis=1)
    w_out_bf = w_out.astype(BF16)
    w_glu_bf = s5_w_glu.astype(BF16)
    w1_bf, w3_bf, w2_bf = moe_w1.astype(BF16), moe_w3.astype(BF16), moe_w2.astype(BF16)
    w_r = jnp.zeros((depth, d, LANES), F32).at[:, :, :N_GROUPS].set(moe_w_group)
    w_r = w_r.at[:, :, N_GROUPS:N_GROUPS + N_EXPERTS].set(moe_w_expert)
    rb = jnp.zeros((depth, 1, LANES), F32).at[:, 0, :N_GROUPS].set(moe_b_group)
    rb = rb.at[:, 0, N_GROUPS:N_GROUPS + N_EXPERTS].set(moe_b_expert)
    wr_hi, wr_lo = _split_bf16(w_r)
    na_bias = _na_bias(na_rpb, n_lat // GRID_W)

    cos_t, sin_t = _rope_tables(n_lat)
    tile2 = lambda g: jnp.tile(g.astype(F32), 2).reshape(1, LANES)
    xs = jnp.concatenate([ctx, x], axis=1)

    for l in range(depth):
        lam_init = 0.8 - 0.6 * math.exp(-0.3 * l)
        mt = modtab[l]

        h = _norm(xs, norm_mix[l], mt, 0, 1).reshape(ntok, d)
        p1 = _matmul(h, w_p1, l, BF16)
        gates = _matmul(h, w_gate, l, BF16)
        p1_3d = p1.reshape(bsz, tlen, P1_WIDTH)

        prm = _s5_params(s5_a_re[l], s5_a_im[l], s5_log_step[l], s5_b_re[l], s5_b_im[l],
                         s5_c_re[l], s5_c_im[l])
        y_s5 = _s5_scan(_s5_pack(p1_3d), prm, bsz)
        y_a = _s5_glu(y_s5, p1_3d, s5_d[l], w_glu_bf, l, s5_b_glu[l]).reshape(ntok, BRANCH_W)

        y_b = _win_attn(p1_3d, win_sink[l].astype(F32), cos_t, sin_t,
                        tile2(win_qn[l]), tile2(win_kn[l])).reshape(ntok, BRANCH_W)

        lp = diff_lambda[l].astype(F32)
        lam = jnp.exp(jnp.sum(lp[0] * lp[1])) - jnp.exp(jnp.sum(lp[2] * lp[3])) + lam_init
        lam_vec = jnp.stack([lam, jnp.asarray(1.0 - lam_init, F32)])
        y_c = _diff_attn(p1_3d, lam_vec, cos_t, sin_t, tile2(diff_qn[l]), tile2(diff_kn[l]),
                         diff_subln[l].astype(F32).reshape(1, LANES)).reshape(ntok, BRANCH_W)

        y_d = _na_attn(p1_3d, na_bias, l, tile2(na_qn[l]), tile2(na_kn[l])).reshape(ntok, BRANCH_W)

        merged = _merge((y_a, y_b, y_c, y_d), gates, wb_all, l)
        xs = _matmul_resid(merged, w_out_bf, l, xs.reshape(ntok, d), mt, 2, bpb)
        xs = xs.reshape(bsz, tlen, d)

        hf, logits = _norm_router(xs, norm_ffn[l], mt, 3, 4, wr_hi[l], wr_lo[l], rb[l])
        te, rt, na, rw, d0, d1 = _route(logits.reshape(ntok, LANES))
        ys = _moe_experts(hf.reshape(ntok, d), te, rt, na, rw, w1_bf, w3_bf, w2_bf, l)
        xs = _moe_combine(ys, d0, d1, xs.reshape(ntok, d), mt, 5, bpb).reshape(bsz, tlen, d)

    return xs[:, CTX_LEN:]
```

```python
import functools
import math

import jax
import jax.numpy as jnp
from jax import lax
from jax.experimental import pallas as pl
from jax.experimental.pallas import tpu as pltpu

F32 = jnp.float32
BF16 = jnp.bfloat16

D_MODEL = 2048
DEPTH = 4
GRID_W = 64
CTX_LEN = 256
HEAD_DIM = 64
BRANCH_W = 512
N_BRANCH = 4
S5_CH = 16
S5_GROUPS = BRANCH_W // S5_CH
S5_STATE = 64
WIN_HEADS = 8
WIN_KV_HEADS = 2
WINDOW = 128
DIFF_HEADS = 4
NA_HEADS = 8
NA_WIN_R = 8
NA_WIN_C = 16
N_GROUPS = 4
EXPERTS_PER_GROUP = 4
N_EXPERTS = 16
EXPERT_FF = D_MODEL // 2
ROPE_BASE = 100.0
EPS = 1e-6
NEG_INF = -1e30

LANES = 128
SEG_ROWS = 256
ADA_ROWS = 16
P1_WIDTH = 9 * BRANCH_W
MOE_TM = 256
VMEM_LIMIT = 52 * 1024 * 1024

S5_CHUNK = 16
S5_PACK_ROWS = 128
S5_NPAIR = S5_GROUPS // 2
S5_PIECE = 2 * S5_CH

LOG2E = 1.4426950408889634
SCALE = HEAD_DIM ** -0.5


def _cp(sem, vmem=VMEM_LIMIT):
    return pltpu.CompilerParams(dimension_semantics=sem, vmem_limit_bytes=vmem)


def _dot(a, b):
    return jnp.dot(a, b, preferred_element_type=F32)


def _dot_t(a, b):
    return lax.dot_general(a, b, (((1,), (1,)), ((), ())), preferred_element_type=F32)


def _split_bf16(x):
    hi = x.astype(BF16)
    lo = (x - hi.astype(F32)).astype(BF16)
    return hi, lo


def _sigmoid(x):
    return 0.5 * jnp.tanh(0.5 * x) + 0.5


def _rowsum(x):
    acc = x[:, :LANES]
    for j in range(1, x.shape[1] // LANES):
        acc = acc + x[:, j * LANES:(j + 1) * LANES]
    return jnp.sum(acc, axis=-1, keepdims=True)


def _ada_kernel(c_ref, w_ref, b_ref, o_ref):
    c = c_ref[...]
    a_hi, a_lo = _split_bf16(c * jax.nn.sigmoid(c))
    w_hi, w_lo = _split_bf16(w_ref[0])
    o_ref[0] = _dot(a_hi, w_hi) + _dot(a_lo, w_hi) + _dot(a_hi, w_lo) + b_ref[0]


def _ada_mod(cond, w_ada, b_ada):
    depth, d, n = w_ada.shape
    tn = 1024
    return pl.pallas_call(
        _ada_kernel,
        grid=(depth, n // tn),
        in_specs=[pl.BlockSpec((ADA_ROWS, d), lambda l, j: (0, 0)),
                  pl.BlockSpec((1, d, tn), lambda l, j: (l, 0, j)),
                  pl.BlockSpec((1, 1, tn), lambda l, j: (l, 0, j))],
        out_specs=pl.BlockSpec((1, ADA_ROWS, tn), lambda l, j: (l, 0, j)),
        out_shape=jax.ShapeDtypeStruct((depth, ADA_ROWS, n), F32),
        compiler_params=_cp(("arbitrary", "arbitrary")),
        name="ada_mod",
    )(cond, w_ada, b_ada.reshape(depth, 1, n))


def _mod_norm(x, gain, sh, sc):
    y = x * lax.rsqrt(jnp.mean(x * x, axis=-1, keepdims=True) + EPS) * gain
    return y * (1.0 + sc) + sh


def _norm_kernel(x_ref, g_ref, sh_ref, sc_ref, o_ref):
    o_ref[0] = _mod_norm(x_ref[0], g_ref[...], sh_ref[0], sc_ref[0]).astype(o_ref.dtype)


def _norm_router_kernel(x_ref, g_ref, sh_ref, sc_ref, whi_ref, wlo_ref, rb_ref, o_ref, lg_ref):
    h = _mod_norm(x_ref[0], g_ref[...], sh_ref[0], sc_ref[0])
    o_ref[0] = h
    h_hi, h_lo = _split_bf16(h)
    lg_ref[0] = (_dot(h_hi, whi_ref[...]) + _dot(h_lo, whi_ref[...]) + _dot(h_hi, wlo_ref[...])
                 + rb_ref[...])


def _norm_specs(d, k_shift, k_scale):
    def mod_spec(k):
        return pl.BlockSpec((1, 1, d), lambda b, t: (2 * b + jnp.minimum(t, 1), 0, k))
    return [pl.BlockSpec((1, SEG_ROWS, d), lambda b, t: (b, t, 0)),
            pl.BlockSpec((1, d), lambda b, t: (0, 0)),
            mod_spec(k_shift), mod_spec(k_scale)]


def _norm(xs, gain, modtab, k_shift, k_scale):
    bsz, tlen, d = xs.shape
    return pl.pallas_call(
        _norm_kernel,
        grid=(bsz, tlen // SEG_ROWS),
        in_specs=_norm_specs(d, k_shift, k_scale),
        out_specs=pl.BlockSpec((1, SEG_ROWS, d), lambda b, t: (b, t, 0)),
        out_shape=jax.ShapeDtypeStruct((bsz, tlen, d), BF16),
        compiler_params=_cp(("parallel", "parallel")),
        name="mod_norm",
    )(xs, gain.reshape(1, d), modtab, modtab)


def _norm_router(xs, gain, modtab, k_shift, k_scale, wr_hi, wr_lo, rb):
    bsz, tlen, d = xs.shape
    blk = pl.BlockSpec((1, SEG_ROWS, d), lambda b, t: (b, t, 0))
    const = lambda b, t: (0, 0)
    return pl.pallas_call(
        _norm_router_kernel,
        grid=(bsz, tlen // SEG_ROWS),
        in_specs=_norm_specs(d, k_shift, k_scale) + [
            pl.BlockSpec((d, LANES), const), pl.BlockSpec((d, LANES), const),
            pl.BlockSpec((1, LANES), const)],
        out_specs=[blk, pl.BlockSpec((1, SEG_ROWS, LANES), lambda b, t: (b, t, 0))],
        out_shape=[jax.ShapeDtypeStruct((bsz, tlen, d), F32),
                   jax.ShapeDtypeStruct((bsz, tlen, LANES), F32)],
        compiler_params=_cp(("parallel", "parallel")),
        name="mod_norm_router",
    )(xs, gain.reshape(1, d), modtab, modtab, wr_hi, wr_lo, rb)


def _mm_kernel(a_ref, w_ref, o_ref):
    o_ref[...] = _dot(a_ref[...], w_ref[0]).astype(o_ref.dtype)


def _matmul(a, w_all, layer, out_dtype, tm=1024, tn=512):
    m, k = a.shape
    n = w_all.shape[2]
    return pl.pallas_call(
        _mm_kernel,
        grid=(m // tm, n // tn),
        in_specs=[pl.BlockSpec((tm, k), lambda i, j: (i, 0)),
                  pl.BlockSpec((1, k, tn), lambda i, j: (layer, 0, j))],
        out_specs=pl.BlockSpec((tm, tn), lambda i, j: (i, j)),
        out_shape=jax.ShapeDtypeStruct((m, n), out_dtype),
        compiler_params=_cp(("parallel", "parallel")),
        name="matmul",
    )(a, w_all)


def _seg_mod_row(block, blocks_per_batch):
    b = block // blocks_per_batch
    return 2 * b + jnp.minimum(block % blocks_per_batch, 1)


def _mm_resid_kernel(a_ref, w_ref, x_ref, mod_ref, o_ref, *, blocks_per_batch):
    acc = _dot(a_ref[...], w_ref[0])
    sub = a_ref.shape[0] // SEG_ROWS
    for s in range(sub):
        row = _seg_mod_row(pl.program_id(0) * sub + s, blocks_per_batch)
        rows = slice(s * SEG_ROWS, (s + 1) * SEG_ROWS)
        o_ref[rows, :] = x_ref[rows, :] + mod_ref[row] * acc[rows, :]


def _matmul_resid(a, w_all, layer, x, modtab, k_mod, blocks_per_batch, tm=1024, tn=512):
    m, k = a.shape
    n = w_all.shape[2]
    nrow = modtab.shape[0]
    return pl.pallas_call(
        functools.partial(_mm_resid_kernel, blocks_per_batch=blocks_per_batch),
        grid=(m // tm, n // tn),
        in_specs=[pl.BlockSpec((tm, k), lambda i, j: (i, 0)),
                  pl.BlockSpec((1, k, tn), lambda i, j: (layer, 0, j)),
                  pl.BlockSpec((tm, tn), lambda i, j: (i, j)),
                  pl.BlockSpec((nrow, 1, tn), lambda i, j: (0, 0, k_mod * (n // tn) + j))],
        out_specs=pl.BlockSpec((tm, tn), lambda i, j: (i, j)),
        out_shape=jax.ShapeDtypeStruct((m, n), F32),
        compiler_params=_cp(("parallel", "parallel")),
        name="matmul_resid",
    )(a, w_all, x, modtab)


def _merge_kernel(ya_ref, yb_ref, yc_ref, yd_ref, ga_ref, gb_ref, gc_ref, gd_ref, wb_ref, o_ref):
    acc = None
    for i, (y_ref, g_ref) in enumerate(((ya_ref, ga_ref), (yb_ref, gb_ref),
                                        (yc_ref, gc_ref), (yd_ref, gd_ref))):
        term = _sigmoid(g_ref[...].astype(F32)) * _dot(y_ref[...], wb_ref[0, i])
        acc = term if acc is None else acc + term
    o_ref[...] = acc.astype(o_ref.dtype)


def _merge(ys, gates, wb_all, layer, tm=1024, tn=512):
    m, bw = ys[0].shape
    n = wb_all.shape[3]
    nj = n // tn
    y_spec = pl.BlockSpec((tm, bw), lambda i, j: (i, 0))
    g_specs = [pl.BlockSpec((tm, tn), functools.partial(lambda i, j, q: (i, q * nj + j), q=q))
               for q in range(N_BRANCH)]
    return pl.pallas_call(
        _merge_kernel,
        grid=(m // tm, nj),
        in_specs=[y_spec] * N_BRANCH + g_specs + [
            pl.BlockSpec((1, N_BRANCH, bw, tn), lambda i, j: (layer, 0, 0, j))],
        out_specs=pl.BlockSpec((tm, tn), lambda i, j: (i, j)),
        out_shape=jax.ShapeDtypeStruct((m, n), BF16),
        compiler_params=_cp(("parallel", "parallel")),
        name="branch_merge",
    )(*ys, gates, gates, gates, gates, wb_all)


def _s5_pack_kernel(x_ref, o_ref, xf, ob):
    nb = x_ref.shape[0]
    cpb = S5_PACK_ROWS // S5_CHUNK
    ppt = LANES // S5_PIECE
    for j in range(BRANCH_W // LANES):
        xf[j] = x_ref[:, :, j * LANES:(j + 1) * LANES].astype(F32)
    for t in range(S5_CHUNK):
        for j in range(BRANCH_W // LANES):
            rows = xf[j, :, pl.ds(t, cpb, stride=S5_CHUNK), :].reshape(nb * cpb, LANES)
            for k in range(ppt):
                ob[j * ppt + k, :, t * S5_PIECE:(t + 1) * S5_PIECE] = rows[:, k * S5_PIECE:(k + 1) * S5_PIECE]
    o_ref[...] = ob[...].astype(o_ref.dtype)


def _s5_pack(p1_3d):
    nb, tlen, _ = p1_3d.shape
    rows = nb * (S5_PACK_ROWS // S5_CHUNK)
    return pl.pallas_call(
        _s5_pack_kernel,
        grid=(tlen // S5_PACK_ROWS,),
        in_specs=[pl.BlockSpec((nb, S5_PACK_ROWS, BRANCH_W), lambda i: (0, i, 0))],
        out_specs=pl.BlockSpec((S5_NPAIR, rows, BRANCH_W), lambda i: (0, i, 0)),
        out_shape=jax.ShapeDtypeStruct((S5_NPAIR, (tlen // S5_PACK_ROWS) * rows, BRANCH_W), BF16),
        scratch_shapes=[pltpu.VMEM((BRANCH_W // LANES, nb, S5_PACK_ROWS, LANES), F32),
                        pltpu.VMEM((S5_NPAIR, rows, BRANCH_W), F32)],
        compiler_params=_cp(("parallel",)),
        name="s5_pack",
    )(p1_3d)


def _s5_kernel(u_ref, ball_ref, toep_ref, call_ref, lam_ref, y_ref, st, *, nb, ctx_chunks):
    u = u_ref[0]
    s_all = _dot(u, ball_ref[0])
    for k in range(4):
        st[k] = s_all[:, k * LANES:(k + 1) * LANES]
    cpb = S5_PACK_ROWS // S5_CHUNK
    nchunk = u.shape[0] // nb
    lam = lam_ref[0]
    lfr, lfi, lbr, lbi = [jnp.broadcast_to(lam[:, k * LANES:(k + 1) * LANES], (nb, LANES)) for k in range(4)]

    def chunk_rows(c):
        return pl.ds((c // cpb) * (cpb * nb) + c % cpb, nb, stride=cpb)

    def step(rows, k, xr, xi, lr, li):
        s_r = st[2 * k, rows, :]
        s_i = st[2 * k + 1, rows, :]
        st[2 * k, rows, :] = xr
        st[2 * k + 1, rows, :] = xi
        return lr * xr - li * xi + s_r, lr * xi + li * xr + s_i

    def body(k, carry):
        fr, fi, br, bi = carry
        fr, fi = step(chunk_rows(k), 0, fr, fi, lfr, lfi)
        cb = jnp.where(k < ctx_chunks, ctx_chunks - 1 - k, nchunk - 1 - (k - ctx_chunks))
        br, bi = step(chunk_rows(cb), 1, br, bi, lbr, lbi)
        return fr, fi, br, bi

    zero = jnp.zeros((nb, LANES), F32)
    lax.fori_loop(0, nchunk, body, (zero, zero, zero, zero))
    states = jnp.concatenate([st[k] for k in range(4)], axis=-1).astype(BF16)
    y_ref[0] = _dot(u, toep_ref[0]) + _dot_t(states, call_ref[0])


def _s5_params(a_re, a_im, log_step, b_re, b_im, c_re, c_im):
    L, P, CH, NP = S5_CHUNK, S5_STATE, S5_CH, S5_NPAIR
    lam = lax.complex(a_re, a_im)
    lam_dt = lam * jnp.exp(log_step)[..., None]
    lam_bar = jnp.exp(lam_dt)
    b_bar = ((lam_bar - 1.0) / lam)[..., None] * lax.complex(b_re, b_im)
    c_mat = lax.complex(c_re, c_im)
    t = jnp.arange(L, dtype=F32)
    mid = float(L // 2)

    def power(z, exps):
        return jnp.exp(lam_dt[z][None] * exps[:, None, None].astype(jnp.complex64))

    def row_side(z, exps):
        return power(z, exps)[:, :, None, :] * jnp.swapaxes(b_bar[z], 1, 2)[None]

    def col_side(z, exps):
        return power(z, exps)[:, :, None, :] * c_mat[z][None]

    gi = lax.broadcasted_iota(jnp.int32, (NP, L, 2, CH, P), 2)

    def pair_mat(x):
        x5 = jnp.transpose(x.reshape(L, NP, 2, CH, P), (1, 0, 2, 3, 4))
        blocks = [jnp.where(gi == h, x5, 0.0) for h in range(2)]
        return jnp.concatenate(blocks, axis=-1).reshape(NP, L * S5_PIECE, 2 * P)

    def cplx_pair(x, conj_sign):
        return jnp.concatenate([pair_mat(x.real), pair_mat(conj_sign * x.imag)], axis=-1)

    hi = lax.Precision.HIGHEST
    fwd = jnp.einsum('qap,qbp->qab', cplx_pair(row_side(0, mid - t), 1.0),
                     cplx_pair(col_side(0, t - mid), -1.0), precision=hi)
    bwd = jnp.einsum('qap,qbp->qab', cplx_pair(row_side(1, t - mid), 1.0),
                     cplx_pair(col_side(1, mid - t), -1.0), precision=hi)
    s_of = lax.broadcasted_iota(jnp.int32, (L * S5_PIECE, L * S5_PIECE), 0) // S5_PIECE
    t_of = lax.broadcasted_iota(jnp.int32, (L * S5_PIECE, L * S5_PIECE), 1) // S5_PIECE
    toep = jnp.where(t_of >= s_of, fwd, 0.0) + jnp.where(s_of >= t_of, bwd, 0.0)

    ball = jnp.concatenate([cplx_pair(row_side(0, L - 1.0 - t), 1.0), cplx_pair(row_side(1, t), 1.0)], axis=-1)
    call_t = jnp.concatenate([cplx_pair(col_side(0, t + 1.0), -1.0), cplx_pair(col_side(1, L - t), -1.0)], axis=-1)

    lam_l = jnp.exp(lam_dt * float(L))
    lam_l = jnp.stack([lam_l[0].real, lam_l[0].imag, lam_l[1].real, lam_l[1].imag])
    lam_l = jnp.transpose(lam_l.reshape(4, NP, 2 * P), (1, 0, 2)).reshape(NP, 1, 8 * P)
    return ball.astype(BF16), toep.astype(BF16), call_t.astype(BF16), lam_l


def _s5_scan(lhs, prm, nb):
    npair, rows, w = lhs.shape
    ball, toep, call, lam_l = prm
    spec_w = pl.BlockSpec((1, w, w), lambda q: (q, 0, 0))
    return pl.pallas_call(
        functools.partial(_s5_kernel, nb=nb, ctx_chunks=CTX_LEN // S5_CHUNK),
        grid=(npair,),
        in_specs=[pl.BlockSpec((1, rows, w), lambda q: (q, 0, 0)), spec_w, spec_w, spec_w,
                  pl.BlockSpec((1, 1, w), lambda q: (q, 0, 0))],
        out_specs=pl.BlockSpec((1, rows, w), lambda q: (q, 0, 0)),
        out_shape=jax.ShapeDtypeStruct((npair, rows, w), F32),
        scratch_shapes=[pltpu.VMEM((4, rows, LANES), F32)],
        compiler_params=_cp(("parallel",)),
        name="s5_scan",
    )(lhs, ball, toep, call, lam_l)


def _glu_kernel(y_ref, u_ref, d_ref, w_ref, b_ref, o_ref, yn):
    nb = u_ref.shape[0]
    cpb = S5_PACK_ROWS // S5_CHUNK
    ppt = LANES // S5_PIECE
    ntile = BRANCH_W // LANES
    for t in range(S5_CHUNK):
        for j in range(ntile):
            tile = jnp.concatenate([y_ref[j * ppt + k, :, t * S5_PIECE:(t + 1) * S5_PIECE]
                                    for k in range(ppt)], axis=-1)
            yn[j, :, pl.ds(t, cpb, stride=S5_CHUNK), :] = tile.reshape(nb, cpb, LANES)
    yd = jnp.concatenate([yn[j] for j in range(ntile)], axis=-1)
    y = d_ref[...] * u_ref[...].astype(F32).reshape(nb * S5_PACK_ROWS, BRANCH_W)
    y = y + yd.reshape(nb * S5_PACK_ROWS, BRANCH_W)
    cdf = 0.5 * (1.0 + jnp.tanh(math.sqrt(2.0 / math.pi) * (y + 0.044715 * (y * y * y))))
    g = y * cdf
    z = _dot(g.astype(BF16), w_ref[0]) + b_ref[...]
    o_ref[...] = (g * _sigmoid(z)).astype(o_ref.dtype).reshape(o_ref.shape)


def _s5_glu(y, p1_3d, d_skip, w_glu_all, layer, b_glu):
    nb, tlen, _ = p1_3d.shape
    bw = BRANCH_W
    rows = nb * (S5_PACK_ROWS // S5_CHUNK)
    const = lambda i: (0, 0)
    blk = pl.BlockSpec((nb, S5_PACK_ROWS, bw), lambda i: (0, i, 0))
    return pl.pallas_call(
        _glu_kernel,
        grid=(tlen // S5_PACK_ROWS,),
        in_specs=[pl.BlockSpec((S5_NPAIR, rows, bw), lambda i: (0, i, 0)), blk,
                  pl.BlockSpec((1, bw), const), pl.BlockSpec((1, bw, bw), lambda i: (layer, 0, 0)),
                  pl.BlockSpec((1, bw), const)],
        out_specs=blk,
        out_shape=jax.ShapeDtypeStruct((nb, tlen, bw), BF16),
        scratch_shapes=[pltpu.VMEM((bw // LANES, nb, S5_PACK_ROWS, LANES), F32)],
        compiler_params=_cp(("parallel",)),
        name="s5_glu",
    )(y, p1_3d, d_skip.reshape(1, bw), w_glu_all, b_glu.reshape(1, bw))


def _lane_masks(rows):
    lane = lax.broadcasted_iota(jnp.int32, (rows, LANES), 1)
    return lane < HEAD_DIM, (lane % HEAD_DIM) < (HEAD_DIM // 2)


def _head_norm(x, gain, lo):
    ss = x * x
    s_lo = jnp.sum(jnp.where(lo, ss, 0.0), axis=-1, keepdims=True)
    s_hi = jnp.sum(jnp.where(lo, 0.0, ss), axis=-1, keepdims=True)
    ms = jnp.where(lo, s_lo, s_hi) * (1.0 / HEAD_DIM)
    return x * lax.rsqrt(ms + EPS) * gain


def _rope(x, cos, sin_signed, first_half):
    partner = jnp.where(first_half, pltpu.roll(x, LANES - HEAD_DIM // 2, 1),
                        pltpu.roll(x, HEAD_DIM // 2, 1))
    return x * cos + partner * sin_signed


def _prep_slab(x, gain, cos, sin_signed, scale):
    rows = x.shape[0]
    lo, first_half = _lane_masks(rows)
    y = _head_norm(x.astype(F32), gain, lo)
    if cos is not None:
        y = _rope(y, cos, sin_signed, first_half)
    if scale is not None:
        y = y * scale
    return y.astype(BF16)


def _prep_keys(k_ref, kn_ref, gain, cos_ref, sin_ref, nslab):
    tlen = k_ref.shape[1]
    for r0 in range(0, tlen, SEG_ROWS):
        rows = slice(r0, r0 + SEG_ROWS)
        for s in range(nslab):
            cols = slice(s * LANES, (s + 1) * LANES)
            cos = None if cos_ref is None else cos_ref[rows, :]
            sin = None if sin_ref is None else sin_ref[rows, :]
            kn_ref[rows, cols] = _prep_slab(k_ref[0, rows, cols], gain, cos, sin, None)


def _prep_values(v_ref, va_ref, nslab):
    tlen = v_ref.shape[1]
    for r0 in range(0, tlen, SEG_ROWS):
        rows = slice(r0, r0 + SEG_ROWS)
        lo, _ = _lane_masks(SEG_ROWS)
        for s in range(nslab):
            cols = slice(s * LANES, (s + 1) * LANES)
            v = v_ref[0, rows, cols]
            one = jnp.ones_like(v)
            va_ref[0, rows, cols] = jnp.where(lo, v, one)
            va_ref[1, rows, cols] = jnp.where(lo, one, v)


def _split_heads(q):
    lo, _ = _lane_masks(q.shape[0])
    zero = jnp.zeros_like(q)
    return jnp.where(lo, q, zero), jnp.where(lo, zero, q), lo


def _join_heads(o_lo, o_hi, extra_lo, extra_hi, lo):
    num = jnp.where(lo, o_lo, o_hi)
    den = pltpu.roll(jnp.where(lo, o_hi, o_lo), HEAD_DIM, 1)
    if extra_lo is not None:
        den = den + jnp.where(lo, extra_lo, extra_hi)
    return num / den


def _win_kernel(sink_ref, q_ref, k_ref, v_ref, cos_ref, sin_ref, qg_ref, kg_ref, o_ref,
                kn_ref, va_ref, qs_ref, e_ref):
    qi = pl.program_id(1)
    tq = q_ref.shape[1]
    tlen = k_ref.shape[1]
    span = tq + 2 * WINDOW
    nhead = WIN_HEADS

    @pl.when(qi == 0)
    def _():
        _prep_keys(k_ref, kn_ref, kg_ref[...], cos_ref, sin_ref, 1)
        _prep_values(v_ref, va_ref, 1)

    qrows = pl.ds(pl.multiple_of(qi * tq, tq), tq)
    cos_q = cos_ref[qrows, :]
    sin_q = sin_ref[qrows, :]

    def attend(local):
        for s in range(nhead // 2):
            cols = slice(s * LANES, (s + 1) * LANES)
            qn = _prep_slab(q_ref[0, :, cols], qg_ref[...], cos_q, sin_q, SCALE * LOG2E)
            q_lo, q_hi, _ = _split_heads(qn)
            qs_ref[(2 * s) * tq:(2 * s + 1) * tq, :] = q_lo
            qs_ref[(2 * s + 1) * tq:(2 * s + 2) * tq, :] = q_hi
        qs = qs_ref[...]
        s_ctx = _dot_t(qs, kn_ref[0:CTX_LEN, :])
        m = jnp.max(s_ctx, axis=-1, keepdims=True)
        if local:
            start = jnp.clip(qi * tq - WINDOW, CTX_LEN, tlen - span)
            krows = pl.ds(pl.multiple_of(start, LANES), span)
            qtok = qi * tq + lax.broadcasted_iota(jnp.int32, (tq, span), 0)
            ktok = start + lax.broadcasted_iota(jnp.int32, (tq, span), 1)
            valid = (jnp.abs(qtok - ktok) <= WINDOW)[None]
            s_loc = _dot_t(qs, kn_ref[krows, :]).reshape(nhead, tq, span)
            s_loc = jnp.where(valid, s_loc, NEG_INF).reshape(nhead * tq, span)
            m = jnp.maximum(m, jnp.max(s_loc, axis=-1, keepdims=True))
        sinks = [sink_ref[(hb // 2) + (hb % 2) * (nhead // 2)] * LOG2E for hb in range(nhead)]
        m = jnp.concatenate([jnp.maximum(m[hb * tq:(hb + 1) * tq], sinks[hb]) for hb in range(nhead)], axis=0)
        e_ref[:, 0:CTX_LEN] = jnp.exp2(s_ctx - m).astype(BF16)
        if local:
            e_ref[:, CTX_LEN:CTX_LEN + span] = jnp.exp2(s_loc - m).astype(BF16)
        lo, _ = _lane_masks(tq)
        for s in range(nhead // 2):
            outs, extra = [], []
            for half in range(2):
                hb = 2 * s + half
                rows = slice(hb * tq, (hb + 1) * tq)
                o = _dot(e_ref[rows, 0:CTX_LEN], va_ref[half, 0:CTX_LEN, :])
                if local:
                    o = o + _dot(e_ref[rows, CTX_LEN:CTX_LEN + span], va_ref[half, krows, :])
                outs.append(o)
                extra.append(jnp.exp2(sinks[hb] - m[rows]))
            o_ref[0, :, s * LANES:(s + 1) * LANES] = _join_heads(
                outs[0], outs[1], extra[0], extra[1], lo).astype(o_ref.dtype)

    @pl.when(qi < CTX_LEN // tq)
    def _():
        attend(False)

    @pl.when(qi >= CTX_LEN // tq)
    def _():
        attend(True)


def _win_attn(p1, sink, cos_t, sin_t, qg, kg, tq=128):
    bsz, tlen, _ = p1.shape
    kcol = 8 * BRANCH_W // LANES
    return pl.pallas_call(
        _win_kernel,
        grid=(bsz, tlen // tq),
        in_specs=[pl.BlockSpec(memory_space=pltpu.SMEM),
                  pl.BlockSpec((1, tq, BRANCH_W), lambda b, i: (b, i, 1)),
                  pl.BlockSpec((1, tlen, LANES), lambda b, i: (b, 0, kcol)),
                  pl.BlockSpec((1, tlen, LANES), lambda b, i: (b, 0, kcol + 1)),
                  pl.BlockSpec((tlen, LANES), lambda b, i: (0, 0)),
                  pl.BlockSpec((tlen, LANES), lambda b, i: (0, 0)),
                  pl.BlockSpec((1, LANES), lambda b, i: (0, 0)),
                  pl.BlockSpec((1, LANES), lambda b, i: (0, 0))],
        out_specs=pl.BlockSpec((1, tq, BRANCH_W), lambda b, i: (b, i, 0)),
        out_shape=jax.ShapeDtypeStruct((bsz, tlen, BRANCH_W), BF16),
        scratch_shapes=[pltpu.VMEM((tlen, LANES), BF16), pltpu.VMEM((2, tlen, LANES), BF16),
                        pltpu.VMEM((WIN_HEADS * tq, LANES), BF16),
                        pltpu.VMEM((WIN_HEADS * tq, CTX_LEN + tq + 2 * WINDOW), BF16)],
        compiler_params=_cp(("parallel", "arbitrary")),
        name="win_attn",
    )(sink, p1, p1, p1, cos_t, sin_t, qg, kg)


def _diff_kernel(lam_ref, q_ref, k_ref, v_ref, cos_ref, sin_ref, qg_ref, kg_ref, sg_ref, o_ref, kn_ref):
    qi = pl.program_id(1)
    tq = q_ref.shape[1]
    tlen = k_ref.shape[1]
    lam = lam_ref[0]
    out_scale = lam_ref[1]

    @pl.when(qi == 0)
    def _():
        _prep_keys(k_ref, kn_ref, kg_ref[...], cos_ref, sin_ref, DIFF_HEADS)

    qrows = pl.ds(pl.multiple_of(qi * tq, tq), tq)
    cos_q = cos_ref[qrows, :]
    sin_q = sin_ref[qrows, :]

    def attend(nk):
        for h in range(DIFF_HEADS):
            cols = slice(h * LANES, (h + 1) * LANES)
            qn = _prep_slab(q_ref[0, :, cols], qg_ref[...], cos_q, sin_q, SCALE * LOG2E)
            q1, q2, _ = _split_heads(qn)
            keys = kn_ref[0:nk, cols]
            vals = v_ref[0, 0:nk, cols]
            s1 = _dot_t(q1, keys)
            s2 = _dot_t(q2, keys)
            e1 = jnp.exp2(s1 - jnp.max(s1, axis=-1, keepdims=True))
            e2 = jnp.exp2(s2 - jnp.max(s2, axis=-1, keepdims=True))
            r1 = 1.0 / _rowsum(e1)
            r2 = lam / _rowsum(e2)
            o = _dot(e1.astype(BF16), vals) * r1 - _dot(e2.astype(BF16), vals) * r2
            o = o * lax.rsqrt(jnp.mean(o * o, axis=-1, keepdims=True) + EPS) * sg_ref[...]
            o_ref[0, :, cols] = (o * out_scale).astype(o_ref.dtype)

    @pl.when(qi < CTX_LEN // tq)
    def _():
        attend(CTX_LEN)

    @pl.when(qi >= CTX_LEN // tq)
    def _():
        attend(tlen)


def _diff_attn(p1, lam_vec, cos_t, sin_t, qg, kg, sg, tq=256):
    bsz, tlen, _ = p1.shape
    full = lambda c: pl.BlockSpec((1, tlen, BRANCH_W), functools.partial(lambda b, i, c: (b, 0, c), c=c))
    const = lambda b, i: (0, 0)
    return pl.pallas_call(
        _diff_kernel,
        grid=(bsz, tlen // tq),
        in_specs=[pl.BlockSpec(memory_space=pltpu.SMEM),
                  pl.BlockSpec((1, tq, BRANCH_W), lambda b, i: (b, i, 2)),
                  full(3), full(4),
                  pl.BlockSpec((tlen, LANES), const), pl.BlockSpec((tlen, LANES), const),
                  pl.BlockSpec((1, LANES), const), pl.BlockSpec((1, LANES), const),
                  pl.BlockSpec((1, LANES), const)],
        out_specs=pl.BlockSpec((1, tq, BRANCH_W), lambda b, i: (b, i, 0)),
        out_shape=jax.ShapeDtypeStruct((bsz, tlen, BRANCH_W), BF16),
        scratch_shapes=[pltpu.VMEM((tlen, BRANCH_W), BF16)],
        compiler_params=_cp(("parallel", "arbitrary")),
        name="diff_attn",
    )(lam_vec, p1, p1, p1, cos_t, sin_t, qg, kg, sg)


def _na_kernel(q_ref, k_ref, v_ref, bias_ref, qg_ref, kg_ref, o_ref, kn_ref, va_ref, qs_ref, sc_ref, e_ref,
               *, grid_rows):
    qi = pl.program_id(1)
    span = NA_WIN_R * GRID_W
    nslab = NA_HEADS // 2

    @pl.when(qi == 0)
    def _():
        _prep_keys(k_ref, kn_ref, kg_ref[...], None, None, nslab)
        _prep_values(v_ref, va_ref, nslab)

    def attend(qrows, nq, krows):
        for s in range(nslab):
            cols = slice(s * LANES, (s + 1) * LANES)
            qn = _prep_slab(q_ref[0, qrows, cols], qg_ref[...], None, None, SCALE * LOG2E)
            q_lo, q_hi, _ = _split_heads(qn)
            qs_ref[(2 * s) * nq:(2 * s + 1) * nq, :] = q_lo
            qs_ref[(2 * s + 1) * nq:(2 * s + 2) * nq, :] = q_hi
        ncol = CTX_LEN if krows is None else CTX_LEN + span
        for s in range(nslab):
            cols = slice(s * LANES, (s + 1) * LANES)
            rows = slice(2 * s * nq, (2 * s + 2) * nq)
            sc_ref[rows, 0:CTX_LEN] = _dot_t(qs_ref[rows, :], kn_ref[0:CTX_LEN, cols])
            if krows is not None:
                bias = bias_ref[0, 2 * s:2 * s + 2].reshape(2 * nq, span)
                sc_ref[rows, CTX_LEN:ncol] = _dot_t(qs_ref[rows, :], kn_ref[krows, cols]) + bias
        nrow = NA_HEADS * nq
        sc = sc_ref[0:nrow, 0:ncol]
        e_ref[0:nrow, 0:ncol] = jnp.exp2(sc - jnp.max(sc, axis=-1, keepdims=True)).astype(BF16)
        lo, _ = _lane_masks(nq)
        for s in range(nslab):
            cols = slice(s * LANES, (s + 1) * LANES)
            outs = []
            for half in range(2):
                rows = slice((2 * s + half) * nq, (2 * s + half + 1) * nq)
                o = _dot(e_ref[rows, 0:CTX_LEN], va_ref[half, 0:CTX_LEN, cols])
                if krows is not None:
                    o = o + _dot(e_ref[rows, CTX_LEN:ncol], va_ref[half, krows, cols])
                outs.append(o)
            o_ref[0, qrows, cols] = _join_heads(outs[0], outs[1], None, None, lo).astype(o_ref.dtype)

    @pl.when(qi == 0)
    def _():
        attend(slice(0, CTX_LEN), CTX_LEN, None)

    @pl.when(qi > 0)
    def _():
        r = qi - 1
        qrows = pl.ds(pl.multiple_of(CTX_LEN + r * GRID_W, GRID_W), GRID_W)
        k0 = jnp.clip(r - NA_WIN_R // 2, 0, grid_rows - NA_WIN_R)
        krows = pl.ds(pl.multiple_of(CTX_LEN + k0 * GRID_W, GRID_W), span)
        attend(qrows, GRID_W, krows)


def _na_bias(rpb_all, rows):
    win_r = NA_WIN_R
    half = win_r // 2
    r = jnp.concatenate([jnp.arange(half + 1), jnp.arange(rows - half + 1, rows)])
    row_idx = jnp.clip(r - half, 0, rows - win_r)[:, None] + jnp.arange(win_r)[None, :]
    r_off = row_idx - r[:, None] + (NA_WIN_R - 1)
    col = jnp.arange(GRID_W)
    c_off = jnp.clip(col[None, :] - col[:, None] + (NA_WIN_C - 1), 0, 2 * NA_WIN_C - 2)
    onehot = (c_off[None] == jnp.arange(2 * NA_WIN_C - 1)[:, None, None]).astype(F32)
    rows_sel = rpb_all.astype(F32)[:, :, r_off] * LOG2E
    bias = jnp.einsum('lhpwk,kqc->lphqwc', rows_sel, onehot, precision=lax.Precision.HIGHEST)
    col_start = jnp.clip(col - NA_WIN_C // 2, 0, GRID_W - NA_WIN_C)
    col_ok = (col[None, :] >= col_start[:, None]) & (col[None, :] < col_start[:, None] + NA_WIN_C)
    bias = jnp.where(col_ok[None, None, None, :, None, :], bias, NEG_INF)
    depth = rpb_all.shape[0]
    return bias.reshape(depth, r.shape[0], NA_HEADS, GRID_W, win_r * GRID_W)


def _na_attn(p1, bias_all, layer, qg, kg):
    bsz, tlen, _ = p1.shape
    rows = (tlen - CTX_LEN) // GRID_W
    half = NA_WIN_R // 2
    full = lambda c: pl.BlockSpec((1, tlen, BRANCH_W), functools.partial(lambda b, i, c: (b, 0, c), c=c))
    const = lambda b, i: (0, 0)

    def bias_idx(b, i):
        r = jnp.maximum(i - 1, 0)
        return (layer, jnp.minimum(r, half) + jnp.maximum(r - (rows - half), 0), 0, 0, 0)

    return pl.pallas_call(
        functools.partial(_na_kernel_wrap, grid_rows=rows),
        grid=(bsz, rows + 1),
        in_specs=[full(5), full(6), full(7),
                  pl.BlockSpec((1, 1, NA_HEADS, GRID_W, NA_WIN_R * GRID_W), bias_idx),
                  pl.BlockSpec((1, LANES), const), pl.BlockSpec((1, LANES), const)],
        out_specs=pl.BlockSpec((1, tlen, BRANCH_W), lambda b, i: (b, 0, 0)),
        out_shape=jax.ShapeDtypeStruct((bsz, tlen, BRANCH_W), BF16),
        scratch_shapes=[pltpu.VMEM((tlen, BRANCH_W), BF16), pltpu.VMEM((2, tlen, BRANCH_W), BF16),
                        pltpu.VMEM((NA_HEADS * CTX_LEN, LANES), BF16),
                        pltpu.VMEM((NA_HEADS * CTX_LEN, CTX_LEN + NA_WIN_R * GRID_W), F32),
                        pltpu.VMEM((NA_HEADS * CTX_LEN, CTX_LEN + NA_WIN_R * GRID_W), BF16)],
        compiler_params=_cp(("parallel", "arbitrary")),
        name="na_attn",
    )(p1, p1, p1, bias_all, qg, kg)


def _na_kernel_wrap(q_ref, k_ref, v_ref, bias_ref, *rest, grid_rows):
    _na_kernel(q_ref, k_ref, v_ref, bias_ref.at[0], *rest, grid_rows=grid_rows)


def _row_copy(src_hbm, row, dst, r, sem):
    return pltpu.make_async_copy(src_hbm.at[pl.ds(row, 1), :], dst.at[pl.ds(r, 1), :], sem)


def _gather_rows(idx_ref, base, src_hbm, dst, sem, nrows):
    for r in range(nrows):
        _row_copy(src_hbm, idx_ref[base + r], dst, r, sem).start()


def _gather_wait(src_hbm, dst, sem, nrows):
    pltpu.make_async_copy(src_hbm.at[pl.ds(0, nrows), :], dst, sem).wait()


def _expert_kernel(te_ref, rt_ref, na_ref, h_hbm, rw_ref, w1_ref, w3_ref, w2_ref, o_ref, buf, sem):
    i = pl.program_id(0)
    tm = buf.shape[1]
    slot = i % 2
    nact = jnp.maximum(na_ref[0], 1)

    @pl.when(i == 0)
    def _():
        _gather_rows(rt_ref, 0, h_hbm, buf.at[0], sem.at[0], tm)

    @pl.when(i < nact)
    def _():
        _gather_wait(h_hbm, buf.at[slot], sem.at[slot], tm)
        nxt = jnp.minimum(i + 1, pl.num_programs(0) - 1)
        _gather_rows(rt_ref, nxt * tm, h_hbm, buf.at[1 - slot], sem.at[1 - slot], tm)
        x = buf[slot].astype(BF16)
        a = _dot(x, w1_ref[0, 0])
        mid = (a * jax.nn.sigmoid(a)) * _dot(x, w3_ref[0, 0])
        o_ref[...] = _dot(mid.astype(BF16), w2_ref[0, 0]) * rw_ref[...]

    @pl.when(i == nact - 1)
    def _():
        _gather_wait(h_hbm, buf.at[1 - slot], sem.at[1 - slot], tm)

    @pl.when(i >= nact)
    def _():
        o_ref[...] = jnp.zeros_like(o_ref)


def _moe_experts(h, tile_expert, row_token, n_active, row_weight, w1_all, w3_all, w2_all, layer, tm=MOE_TM):
    n, d = h.shape
    rmax = row_token.shape[0]
    ff = w1_all.shape[3]
    grid_spec = pltpu.PrefetchScalarGridSpec(
        num_scalar_prefetch=3,
        grid=(rmax // tm,),
        in_specs=[pl.BlockSpec(memory_space=pl.ANY),
                  pl.BlockSpec((tm, 1), lambda i, te, rt, na: (i, 0)),
                  pl.BlockSpec((1, 1, d, ff), lambda i, te, rt, na: (layer, te[i], 0, 0)),
                  pl.BlockSpec((1, 1, d, ff), lambda i, te, rt, na: (layer, te[i], 0, 0)),
                  pl.BlockSpec((1, 1, ff, d), lambda i, te, rt, na: (layer, te[i], 0, 0))],
        out_specs=pl.BlockSpec((tm, d), lambda i, te, rt, na: (i, 0)),
        scratch_shapes=[pltpu.VMEM((2, tm, d), F32), pltpu.SemaphoreType.DMA((2,))])
    return pl.pallas_call(
        _expert_kernel,
        grid_spec=grid_spec,
        out_shape=jax.ShapeDtypeStruct((rmax, d), F32),
        compiler_params=_cp(("arbitrary",)),
        name="moe_experts",
    )(tile_expert, row_token, n_active, h, row_weight, w1_all, w3_all, w2_all)


def _combine_kernel(d0_ref, d1_ref, ys_hbm, x_ref, mod_ref, o_ref, buf0, buf1, sem,
                    *, blocks_per_batch):
    i = pl.program_id(0)
    n = pl.num_programs(0)
    tm = x_ref.shape[0]
    slot = i % 2

    def issue(tile, s):
        _gather_rows(d0_ref, tile * tm, ys_hbm, buf0.at[s], sem.at[0, s], tm)
        _gather_rows(d1_ref, tile * tm, ys_hbm, buf1.at[s], sem.at[1, s], tm)

    @pl.when(i == 0)
    def _():
        issue(0, 0)

    @pl.when(i + 1 < n)
    def _():
        issue(i + 1, 1 - slot)

    _gather_wait(ys_hbm, buf0.at[slot], sem.at[0, slot], tm)
    _gather_wait(ys_hbm, buf1.at[slot], sem.at[1, slot], tm)
    row = _seg_mod_row(i, blocks_per_batch)
    o_ref[...] = x_ref[...] + mod_ref[row] * (buf0[slot] + buf1[slot])


def _moe_combine(ys, dest0, dest1, x, modtab, k_mod, blocks_per_batch):
    n, d = x.shape
    tm = SEG_ROWS
    nrow = modtab.shape[0]
    grid_spec = pltpu.PrefetchScalarGridSpec(
        num_scalar_prefetch=2,
        grid=(n // tm,),
        in_specs=[pl.BlockSpec(memory_space=pl.ANY),
                  pl.BlockSpec((tm, d), lambda i, a, b: (i, 0)),
                  pl.BlockSpec((nrow, 1, d), lambda i, a, b: (0, 0, k_mod))],
        out_specs=pl.BlockSpec((tm, d), lambda i, a, b: (i, 0)),
        scratch_shapes=[pltpu.VMEM((2, tm, d), F32), pltpu.VMEM((2, tm, d), F32),
                        pltpu.SemaphoreType.DMA((2, 2))])
    return pl.pallas_call(
        functools.partial(_combine_kernel, blocks_per_batch=blocks_per_batch),
        grid_spec=grid_spec,
        out_shape=jax.ShapeDtypeStruct((n, d), F32),
        compiler_params=_cp(("arbitrary",)),
        name="moe_combine",
    )(dest0, dest1, ys, x, modtab)


def _route(logits, tm=MOE_TM):
    n = logits.shape[0]
    g_prob = jax.nn.softmax(logits[:, :N_GROUPS], axis=-1)
    g_idx = jnp.argmax(g_prob, axis=-1).astype(jnp.int32)
    g_w = jnp.max(g_prob, axis=-1)
    e_logits = logits[:, N_GROUPS:N_GROUPS + N_EXPERTS].reshape(n, N_GROUPS, EXPERTS_PER_GROUP)
    in_group = jnp.arange(N_GROUPS, dtype=jnp.int32)[None, :, None] == g_idx[:, None, None]
    e_in = jnp.sum(jnp.where(in_group, e_logits, 0.0), axis=1)
    slot = jnp.arange(EXPERTS_PER_GROUP, dtype=jnp.int32)[None, :]
    i1 = jnp.argmax(e_in, axis=-1).astype(jnp.int32)
    v1 = jnp.max(e_in, axis=-1)
    rest = jnp.where(slot == i1[:, None], -jnp.inf, e_in)
    i2 = jnp.argmax(rest, axis=-1).astype(jnp.int32)
    v2 = jnp.max(rest, axis=-1)
    w_sel = jax.nn.softmax(jnp.stack([v1, v2], axis=-1), axis=-1) * g_w[:, None]
    expert = g_idx[:, None] * EXPERTS_PER_GROUP + jnp.stack([i1, i2], axis=-1)

    e_flat = expert.reshape(-1)
    w_flat = w_sel.reshape(-1)
    ids = jnp.arange(N_EXPERTS, dtype=jnp.int32)
    onehot = (e_flat[:, None] == ids[None, :]).astype(jnp.int32)
    csum = jnp.cumsum(onehot, axis=0)
    rank = jnp.sum(onehot * csum, axis=1) - 1
    counts = csum[-1]
    padded = ((counts + tm - 1) // tm) * tm
    pend = jnp.cumsum(padded)
    pstart = pend - padded
    ustart = jnp.cumsum(counts) - counts
    dest = (jnp.sum(onehot * pstart[None, :], axis=1) + rank).astype(jnp.int32)

    rmax = 2 * n + N_EXPERTS * tm
    total = pend[-1]
    tile_start = jnp.arange(rmax // tm, dtype=jnp.int32) * tm
    tile_expert = jnp.minimum(jnp.sum((tile_start[:, None] >= pend[None, :]).astype(jnp.int32), axis=1),
                              N_EXPERTS - 1)
    order = jnp.argsort(e_flat, stable=True).astype(jnp.int32)
    r = jnp.arange(rmax, dtype=jnp.int32)
    e_r = jnp.minimum(jnp.sum((r[:, None] >= pend[None, :]).astype(jnp.int32), axis=1), N_EXPERTS - 1)
    oh_r = (e_r[:, None] == ids[None, :]).astype(jnp.int32)
    off = r - jnp.sum(oh_r * pstart[None, :], axis=1)
    valid = (r < total) & (off < jnp.sum(oh_r * counts[None, :], axis=1))
    src = order[jnp.clip(jnp.sum(oh_r * ustart[None, :], axis=1) + off, 0, 2 * n - 1)]
    row_token = jnp.where(valid, src // 2, 0).astype(jnp.int32)
    row_weight = jnp.where(valid, w_flat[src], 0.0)
    n_active = (total // tm).astype(jnp.int32)
    last_expert = tile_expert[jnp.maximum(n_active - 1, 0)]
    tile_expert = jnp.where(tile_start < total, tile_expert, last_expert).astype(jnp.int32)
    dest2 = dest.reshape(n, 2)
    return (tile_expert, row_token, n_active.reshape(1), row_weight.reshape(rmax, 1),
            dest2[:, 0], dest2[:, 1])


def _rope_tables(n_lat):
    t = jnp.arange(n_lat, dtype=jnp.int32)
    row = (t // GRID_W).astype(F32)
    col = (t % GRID_W).astype(F32)
    per_axis = HEAD_DIM // 4
    inv_freq = ROPE_BASE ** (-jnp.arange(per_axis, dtype=F32) / per_axis)
    ang = jnp.concatenate([row[:, None] * inv_freq, col[:, None] * inv_freq], axis=-1)
    cos = jnp.concatenate([jnp.ones((CTX_LEN, HEAD_DIM // 2), F32), jnp.cos(ang)], axis=0)
    sin = jnp.concatenate([jnp.zeros((CTX_LEN, HEAD_DIM // 2), F32), jnp.sin(ang)], axis=0)
    cos_t = jnp.tile(cos, (1, 4))
    sin_t = jnp.tile(jnp.concatenate([-sin, sin], axis=-1), (1, 2))
    return cos_t, sin_t


def _win_head_perm():
    cols = []
    for s in range(WIN_HEADS // 2):
        for half in range(2):
            h = s + half * (WIN_HEADS // 2)
            cols.extend(range(h * HEAD_DIM, (h + 1) * HEAD_DIM))
    return cols


def _p1_weight(w_in_bf):
    bw = BRANCH_W
    kvw = WIN_KV_HEADS * HEAD_DIM
    hd = HEAD_DIM
    q0 = bw
    q_b = [w_in_bf[:, :, q0 + h * hd:q0 + (h + 1) * hd]
           for s in range(WIN_HEADS // 2) for h in (s, s + WIN_HEADS // 2)]
    kv0 = 2 * bw
    rest0 = kv0 + 2 * kvw
    pad = jnp.zeros(w_in_bf.shape[:2] + (bw - 2 * kvw,), w_in_bf.dtype)
    p1 = jnp.concatenate([w_in_bf[:, :, :bw]] + q_b + [w_in_bf[:, :, rest0:rest0 + 6 * bw],
                                                       w_in_bf[:, :, kv0:rest0], pad], axis=2)
    return p1, w_in_bf[:, :, rest0 + 6 * bw:]


def kernel(x, c, ctx, c_ctx, w_ada, b_ada, norm_mix, norm_ffn, w_in, s5_a_re, s5_a_im, s5_log_step, s5_b_re, s5_b_im, s5_c_re, s5_c_im, s5_d, s5_w_glu, s5_b_glu, win_qn, win_kn, win_sink, diff_qn, diff_kn, diff_lambda, diff_subln, na_qn, na_kn, na_rpb, w_branch, w_out, moe_w_group, moe_b_group, moe_w_expert, moe_b_expert, moe_w1, moe_w3, moe_w2):
    bsz, n_lat, d = x.shape
    tlen = CTX_LEN + n_lat
    ntok = bsz * tlen
    bpb = tlen // SEG_ROWS
    depth = w_ada.shape[0]

    cond = jnp.zeros((ADA_ROWS, d), F32).at[:bsz].set(c).at[bsz].set(c_ctx)
    mod = _ada_mod(cond, w_ada, b_ada)
    mod_ctx = jnp.broadcast_to(mod[:, bsz:bsz + 1], (depth, bsz, 6 * d))
    modtab = jnp.stack([mod_ctx, mod[:, :bsz]], axis=2).reshape(depth, 2 * bsz, 1, 6 * d)

    w_p1, w_gate = _p1_weight(w_in.astype(BF16))
    wb_all = w_branch.astype(BF16)
    wb_win = jnp.concatenate([wb_all[:, 1:2, h * HEAD_DIM:(h + 1) * HEAD_DIM]
                              for s in range(WIN_HEADS // 2) for h in (s, s + WIN_HEADS // 2)], axis=2)
    wb_all = jnp.concatenate([wb_all[:, :1], wb_win, wb_all[:, 2:]], axis=1)
    w_out_bf = w_out.astype(BF16)
    w_glu_bf = s5_w_glu.astype(BF16)
    w1_bf, w3_bf, w2_bf = moe_w1.astype(BF16), moe_w3.astype(BF16), moe_w2.astype(BF16)
    w_r = jnp.zeros((depth, d, LANES), F32).at[:, :, :N_GROUPS].set(moe_w_group)
    w_r = w_r.at[:, :, N_GROUPS:N_GROUPS + N_EXPERTS].set(moe_w_expert)
    rb = jnp.zeros((depth, 1, LANES), F32).at[:, 0, :N_GROUPS].set(moe_b_group)
    rb = rb.at[:, 0, N_GROUPS:N_GROUPS + N_EXPERTS].set(moe_b_expert)
    wr_hi, wr_lo = _split_bf16(w_r)
    na_bias = _na_bias(na_rpb, n_lat // GRID_W)

    cos_t, sin_t = _rope_tables(n_lat)
    tile2 = lambda g: jnp.tile(g.astype(F32), 2).reshape(1, LANES)
    xs = jnp.concatenate([ctx, x], axis=1)

    for l in range(depth):
        lam_init = 0.8 - 0.6 * math.exp(-0.3 * l)
        mt = modtab[l]

        h = _norm(xs, norm_mix[l], mt, 0, 1).reshape(ntok, d)
        p1 = _matmul(h, w_p1, l, BF16, tn=P1_WIDTH // 3)
        gates = _matmul(h, w_gate, l, BF16, tn=1024)
        p1_3d = p1.reshape(bsz, tlen, P1_WIDTH)

        prm = _s5_params(s5_a_re[l], s5_a_im[l], s5_log_step[l], s5_b_re[l], s5_b_im[l],
                         s5_c_re[l], s5_c_im[l])
        y_s5 = _s5_scan(_s5_pack(p1_3d), prm, bsz)
        y_a = _s5_glu(y_s5, p1_3d, s5_d[l], w_glu_bf, l, s5_b_glu[l]).reshape(ntok, BRANCH_W)

        y_b = _win_attn(p1_3d, win_sink[l].astype(F32), cos_t, sin_t,
                        tile2(win_qn[l]), tile2(win_kn[l])).reshape(ntok, BRANCH_W)

        lp = diff_lambda[l].astype(F32)
        lam = jnp.exp(jnp.sum(lp[0] * lp[1])) - jnp.exp(jnp.sum(lp[2] * lp[3])) + lam_init
        lam_vec = jnp.stack([lam, jnp.asarray(1.0 - lam_init, F32)])
        y_c = _diff_attn(p1_3d, lam_vec, cos_t, sin_t, tile2(diff_qn[l]), tile2(diff_kn[l]),
                         diff_subln[l].astype(F32).reshape(1, LANES)).reshape(ntok, BRANCH_W)

        y_d = _na_attn(p1_3d, na_bias, l, tile2(na_qn[l]), tile2(na_kn[l])).reshape(ntok, BRANCH_W)

        merged = _merge((y_a, y_b, y_c, y_d), gates, wb_all, l)
        xs = _matmul_resid(merged, w_out_bf, l, xs.reshape(ntok, d), mt, 2, bpb, tn=1024)
        xs = xs.reshape(bsz, tlen, d)

        hf, logits = _norm_router(xs, norm_ffn[l], mt, 3, 4, wr_hi[l], wr_lo[l], rb[l])
        te, rt, na, rw, d0, d1 = _route(logits.reshape(ntok, LANES))
        ys = _moe_experts(hf.reshape(ntok, d), te, rt, na, rw, w1_bf, w3_bf, w2_bf, l)
        xs = _moe_combine(ys, d0, d1, xs.reshape(ntok, d), mt, 5, bpb).reshape(bsz, tlen, d)

    return xs[:, CTX_LEN:]
```

```python
import functools
import math

import jax
import jax.numpy as jnp
from jax import lax
from jax.experimental import pallas as pl
from jax.experimental.pallas import tpu as pltpu

F32 = jnp.float32
BF16 = jnp.bfloat16

D_MODEL = 2048
DEPTH = 4
GRID_W = 64
CTX_LEN = 256
HEAD_DIM = 64
BRANCH_W = 512
N_BRANCH = 4
S5_CH = 16
S5_GROUPS = BRANCH_W // S5_CH
S5_STATE = 64
WIN_HEADS = 8
WIN_KV_HEADS = 2
WINDOW = 128
DIFF_HEADS = 4
NA_HEADS = 8
NA_WIN_R = 8
NA_WIN_C = 16
N_GROUPS = 4
EXPERTS_PER_GROUP = 4
N_EXPERTS = 16
EXPERT_FF = D_MODEL // 2
ROPE_BASE = 100.0
EPS = 1e-6
NEG_INF = -1e30

LANES = 128
SEG_ROWS = 256
ADA_ROWS = 16
P1_WIDTH = 9 * BRANCH_W
MOE_TM = 256
VMEM_LIMIT = 52 * 1024 * 1024

S5_CHUNK = 16
S5_PACK_ROWS = 128
S5_NPAIR = S5_GROUPS // 2
S5_PIECE = 2 * S5_CH

LOG2E = 1.4426950408889634
SCALE = HEAD_DIM ** -0.5


def _cp(sem, vmem=VMEM_LIMIT):
    return pltpu.CompilerParams(dimension_semantics=sem, vmem_limit_bytes=vmem)


def _dot(a, b):
    return jnp.dot(a, b, preferred_element_type=F32)


def _dot_t(a, b):
    return lax.dot_general(a, b, (((1,), (1,)), ((), ())), preferred_element_type=F32)


def _split_bf16(x):
    hi = x.astype(BF16)
    lo = (x - hi.astype(F32)).astype(BF16)
    return hi, lo


def _sigmoid(x):
    return 0.5 * jnp.tanh(0.5 * x) + 0.5


def _rowsum(x):
    acc = x[:, :LANES]
    for j in range(1, x.shape[1] // LANES):
        acc = acc + x[:, j * LANES:(j + 1) * LANES]
    return jnp.sum(acc, axis=-1, keepdims=True)


def _ada_kernel(c_ref, w_ref, b_ref, o_ref):
    c = c_ref[...]
    a_hi, a_lo = _split_bf16(c * jax.nn.sigmoid(c))
    w_hi, w_lo = _split_bf16(w_ref[0])
    o_ref[0] = _dot(a_hi, w_hi) + _dot(a_lo, w_hi) + _dot(a_hi, w_lo) + b_ref[0]


def _ada_mod(cond, w_ada, b_ada):
    depth, d, n = w_ada.shape
    tn = 1024
    return pl.pallas_call(
        _ada_kernel,
        grid=(depth, n // tn),
        in_specs=[pl.BlockSpec((ADA_ROWS, d), lambda l, j: (0, 0)),
                  pl.BlockSpec((1, d, tn), lambda l, j: (l, 0, j)),
                  pl.BlockSpec((1, 1, tn), lambda l, j: (l, 0, j))],
        out_specs=pl.BlockSpec((1, ADA_ROWS, tn), lambda l, j: (l, 0, j)),
        out_shape=jax.ShapeDtypeStruct((depth, ADA_ROWS, n), F32),
        compiler_params=_cp(("arbitrary", "arbitrary")),
        name="ada_mod",
    )(cond, w_ada, b_ada.reshape(depth, 1, n))


def _mod_norm(x, gain, sh, sc):
    y = x * lax.rsqrt(jnp.mean(x * x, axis=-1, keepdims=True) + EPS) * gain
    return y * (1.0 + sc) + sh


def _norm_kernel(x_ref, g_ref, sh_ref, sc_ref, o_ref):
    o_ref[0] = _mod_norm(x_ref[0], g_ref[...], sh_ref[0], sc_ref[0]).astype(o_ref.dtype)


def _norm_router_kernel(x_ref, g_ref, sh_ref, sc_ref, whi_ref, wlo_ref, rb_ref, o_ref, lg_ref):
    h = _mod_norm(x_ref[0], g_ref[...], sh_ref[0], sc_ref[0])
    o_ref[0] = h
    h_hi, h_lo = _split_bf16(h)
    lg_ref[0] = (_dot(h_hi, whi_ref[...]) + _dot(h_lo, whi_ref[...]) + _dot(h_hi, wlo_ref[...])
                 + rb_ref[...])


def _norm_specs(d, k_shift, k_scale):
    def mod_spec(k):
        return pl.BlockSpec((1, 1, d), lambda b, t: (2 * b + jnp.minimum(t, 1), 0, k))
    return [pl.BlockSpec((1, SEG_ROWS, d), lambda b, t: (b, t, 0)),
            pl.BlockSpec((1, d), lambda b, t: (0, 0)),
            mod_spec(k_shift), mod_spec(k_scale)]


def _norm(xs, gain, modtab, k_shift, k_scale):
    bsz, tlen, d = xs.shape
    return pl.pallas_call(
        _norm_kernel,
        grid=(bsz, tlen // SEG_ROWS),
        in_specs=_norm_specs(d, k_shift, k_scale),
        out_specs=pl.BlockSpec((1, SEG_ROWS, d), lambda b, t: (b, t, 0)),
        out_shape=jax.ShapeDtypeStruct((bsz, tlen, d), BF16),
        compiler_params=_cp(("parallel", "parallel")),
        name="mod_norm",
    )(xs, gain.reshape(1, d), modtab, modtab)


def _norm_router(xs, gain, modtab, k_shift, k_scale, wr_hi, wr_lo, rb):
    bsz, tlen, d = xs.shape
    blk = pl.BlockSpec((1, SEG_ROWS, d), lambda b, t: (b, t, 0))
    const = lambda b, t: (0, 0)
    return pl.pallas_call(
        _norm_router_kernel,
        grid=(bsz, tlen // SEG_ROWS),
        in_specs=_norm_specs(d, k_shift, k_scale) + [
            pl.BlockSpec((d, LANES), const), pl.BlockSpec((d, LANES), const),
            pl.BlockSpec((1, LANES), const)],
        out_specs=[blk, pl.BlockSpec((1, SEG_ROWS, LANES), lambda b, t: (b, t, 0))],
        out_shape=[jax.ShapeDtypeStruct((bsz, tlen, d), F32),
                   jax.ShapeDtypeStruct((bsz, tlen, LANES), F32)],
        compiler_params=_cp(("parallel", "parallel")),
        name="mod_norm_router",
    )(xs, gain.reshape(1, d), modtab, modtab, wr_hi, wr_lo, rb)


def _mm_kernel(a_ref, w_ref, o_ref):
    o_ref[...] = _dot(a_ref[...], w_ref[0]).astype(o_ref.dtype)


def _matmul(a, w_all, layer, out_dtype, tm=1024, tn=512):
    m, k = a.shape
    n = w_all.shape[2]
    return pl.pallas_call(
        _mm_kernel,
        grid=(m // tm, n // tn),
        in_specs=[pl.BlockSpec((tm, k), lambda i, j: (i, 0)),
                  pl.BlockSpec((1, k, tn), lambda i, j: (layer, 0, j))],
        out_specs=pl.BlockSpec((tm, tn), lambda i, j: (i, j)),
        out_shape=jax.ShapeDtypeStruct((m, n), out_dtype),
        compiler_params=_cp(("parallel", "parallel")),
        name="matmul",
    )(a, w_all)


def _seg_mod_row(block, blocks_per_batch):
    b = block // blocks_per_batch
    return 2 * b + jnp.minimum(block % blocks_per_batch, 1)


def _mm_resid_kernel(a_ref, w_ref, x_ref, mod_ref, o_ref, *, blocks_per_batch):
    acc = _dot(a_ref[...], w_ref[0])
    sub = a_ref.shape[0] // SEG_ROWS
    for s in range(sub):
        row = _seg_mod_row(pl.program_id(0) * sub + s, blocks_per_batch)
        rows = slice(s * SEG_ROWS, (s + 1) * SEG_ROWS)
        o_ref[rows, :] = x_ref[rows, :] + mod_ref[row] * acc[rows, :]


def _matmul_resid(a, w_all, layer, x, modtab, k_mod, blocks_per_batch, tm=1024, tn=512):
    m, k = a.shape
    n = w_all.shape[2]
    nrow = modtab.shape[0]
    return pl.pallas_call(
        functools.partial(_mm_resid_kernel, blocks_per_batch=blocks_per_batch),
        grid=(m // tm, n // tn),
        in_specs=[pl.BlockSpec((tm, k), lambda i, j: (i, 0)),
                  pl.BlockSpec((1, k, tn), lambda i, j: (layer, 0, j)),
                  pl.BlockSpec((tm, tn), lambda i, j: (i, j)),
                  pl.BlockSpec((nrow, 1, tn), lambda i, j: (0, 0, k_mod * (n // tn) + j))],
        out_specs=pl.BlockSpec((tm, tn), lambda i, j: (i, j)),
        out_shape=jax.ShapeDtypeStruct((m, n), F32),
        compiler_params=_cp(("parallel", "parallel")),
        name="matmul_resid",
    )(a, w_all, x, modtab)


def _merge_kernel(ya_ref, yb_ref, yc_ref, yd_ref, ga_ref, gb_ref, gc_ref, gd_ref, wb_ref, o_ref):
    acc = None
    for i, (y_ref, g_ref) in enumerate(((ya_ref, ga_ref), (yb_ref, gb_ref),
                                        (yc_ref, gc_ref), (yd_ref, gd_ref))):
        term = _sigmoid(g_ref[...].astype(F32)) * _dot(y_ref[...], wb_ref[0, i])
        acc = term if acc is None else acc + term
    o_ref[...] = acc.astype(o_ref.dtype)


def _merge(ys, gates, wb_all, layer, tm=1024, tn=512):
    m, bw = ys[0].shape
    n = wb_all.shape[3]
    nj = n // tn
    y_spec = pl.BlockSpec((tm, bw), lambda i, j: (i, 0))
    g_specs = [pl.BlockSpec((tm, tn), functools.partial(lambda i, j, q: (i, q * nj + j), q=q))
               for q in range(N_BRANCH)]
    return pl.pallas_call(
        _merge_kernel,
        grid=(m // tm, nj),
        in_specs=[y_spec] * N_BRANCH + g_specs + [
            pl.BlockSpec((1, N_BRANCH, bw, tn), lambda i, j: (layer, 0, 0, j))],
        out_specs=pl.BlockSpec((tm, tn), lambda i, j: (i, j)),
        out_shape=jax.ShapeDtypeStruct((m, n), BF16),
        compiler_params=_cp(("parallel", "parallel")),
        name="branch_merge",
    )(*ys, gates, gates, gates, gates, wb_all)


def _s5_pack_kernel(x_ref, o_ref, xf, ob):
    nb = x_ref.shape[0]
    cpb = S5_PACK_ROWS // S5_CHUNK
    ppt = LANES // S5_PIECE
    for j in range(BRANCH_W // LANES):
        xf[j] = x_ref[:, :, j * LANES:(j + 1) * LANES].astype(F32)
    for t in range(S5_CHUNK):
        for j in range(BRANCH_W // LANES):
            rows = xf[j, :, pl.ds(t, cpb, stride=S5_CHUNK), :].reshape(nb * cpb, LANES)
            for k in range(ppt):
                ob[j * ppt + k, :, t * S5_PIECE:(t + 1) * S5_PIECE] = rows[:, k * S5_PIECE:(k + 1) * S5_PIECE]
    o_ref[...] = ob[...].astype(o_ref.dtype)


def _s5_pack(p1_3d):
    nb, tlen, _ = p1_3d.shape
    rows = nb * (S5_PACK_ROWS // S5_CHUNK)
    return pl.pallas_call(
        _s5_pack_kernel,
        grid=(tlen // S5_PACK_ROWS,),
        in_specs=[pl.BlockSpec((nb, S5_PACK_ROWS, BRANCH_W), lambda i: (0, i, 0))],
        out_specs=pl.BlockSpec((S5_NPAIR, rows, BRANCH_W), lambda i: (0, i, 0)),
        out_shape=jax.ShapeDtypeStruct((S5_NPAIR, (tlen // S5_PACK_ROWS) * rows, BRANCH_W), BF16),
        scratch_shapes=[pltpu.VMEM((BRANCH_W // LANES, nb, S5_PACK_ROWS, LANES), F32),
                        pltpu.VMEM((S5_NPAIR, rows, BRANCH_W), F32)],
        compiler_params=_cp(("parallel",)),
        name="s5_pack",
    )(p1_3d)


def _s5_kernel(u_ref, ball_ref, toep_ref, call_ref, lam_ref, y_ref, st, *, nb, ctx_chunks):
    u = u_ref[0]
    s_all = _dot(u, ball_ref[0, 0])
    for k in range(4):
        st[k] = s_all[:, k * LANES:(k + 1) * LANES]
    cpb = S5_PACK_ROWS // S5_CHUNK
    nchunk = u.shape[0] // nb
    lam = lam_ref[0, 0]
    lfr, lfi, lbr, lbi = [jnp.broadcast_to(lam[:, k * LANES:(k + 1) * LANES], (nb, LANES)) for k in range(4)]

    def chunk_rows(c):
        return pl.ds((c // cpb) * (cpb * nb) + c % cpb, nb, stride=cpb)

    def step(rows, k, xr, xi, lr, li):
        s_r = st[2 * k, rows, :]
        s_i = st[2 * k + 1, rows, :]
        st[2 * k, rows, :] = xr
        st[2 * k + 1, rows, :] = xi
        return lr * xr - li * xi + s_r, lr * xi + li * xr + s_i

    def body(k, carry):
        fr, fi, br, bi = carry
        fr, fi = step(chunk_rows(k), 0, fr, fi, lfr, lfi)
        cb = jnp.where(k < ctx_chunks, ctx_chunks - 1 - k, nchunk - 1 - (k - ctx_chunks))
        br, bi = step(chunk_rows(cb), 1, br, bi, lbr, lbi)
        return fr, fi, br, bi

    zero = jnp.zeros((nb, LANES), F32)
    lax.fori_loop(0, nchunk, body, (zero, zero, zero, zero))
    states = jnp.concatenate([st[k] for k in range(4)], axis=-1).astype(BF16)
    y_ref[0] = _dot(u, toep_ref[0, 0]) + _dot_t(states, call_ref[0, 0])


def _s5_params(a_re, a_im, log_step, b_re, b_im, c_re, c_im):
    L, P, CH, NP = S5_CHUNK, S5_STATE, S5_CH, S5_NPAIR
    lam = lax.complex(a_re, a_im)
    lam_dt = lam * jnp.exp(log_step)[..., None]
    lam_bar = jnp.exp(lam_dt)
    b_bar = ((lam_bar - 1.0) / lam)[..., None] * lax.complex(b_re, b_im)
    c_mat = lax.complex(c_re, c_im)
    t = jnp.arange(L, dtype=F32)
    mid = float(L // 2)

    def power(z, exps):
        return jnp.exp(lam_dt[z][None] * exps[:, None, None].astype(jnp.complex64))

    def row_side(z, exps):
        return power(z, exps)[:, :, None, :] * jnp.swapaxes(b_bar[z], 1, 2)[None]

    def col_side(z, exps):
        return power(z, exps)[:, :, None, :] * c_mat[z][None]

    gi = lax.broadcasted_iota(jnp.int32, (NP, L, 2, CH, P), 2)

    def pair_mat(x):
        x5 = jnp.transpose(x.reshape(L, NP, 2, CH, P), (1, 0, 2, 3, 4))
        blocks = [jnp.where(gi == h, x5, 0.0) for h in range(2)]
        return jnp.concatenate(blocks, axis=-1).reshape(NP, L * S5_PIECE, 2 * P)

    def cplx_pair(x, conj_sign):
        return jnp.concatenate([pair_mat(x.real), pair_mat(conj_sign * x.imag)], axis=-1)

    hi = lax.Precision.HIGHEST
    fwd = jnp.einsum('qap,qbp->qab', cplx_pair(row_side(0, mid - t), 1.0),
                     cplx_pair(col_side(0, t - mid), -1.0), precision=hi)
    bwd = jnp.einsum('qap,qbp->qab', cplx_pair(row_side(1, t - mid), 1.0),
                     cplx_pair(col_side(1, mid - t), -1.0), precision=hi)
    s_of = lax.broadcasted_iota(jnp.int32, (L * S5_PIECE, L * S5_PIECE), 0) // S5_PIECE
    t_of = lax.broadcasted_iota(jnp.int32, (L * S5_PIECE, L * S5_PIECE), 1) // S5_PIECE
    toep = jnp.where(t_of >= s_of, fwd, 0.0) + jnp.where(s_of >= t_of, bwd, 0.0)

    ball = jnp.concatenate([cplx_pair(row_side(0, L - 1.0 - t), 1.0), cplx_pair(row_side(1, t), 1.0)], axis=-1)
    call_t = jnp.concatenate([cplx_pair(col_side(0, t + 1.0), -1.0), cplx_pair(col_side(1, L - t), -1.0)], axis=-1)

    lam_l = jnp.exp(lam_dt * float(L))
    lam_l = jnp.stack([lam_l[0].real, lam_l[0].imag, lam_l[1].real, lam_l[1].imag])
    lam_l = jnp.transpose(lam_l.reshape(4, NP, 2 * P), (1, 0, 2)).reshape(NP, 1, 8 * P)
    return ball.astype(BF16), toep.astype(BF16), call_t.astype(BF16), lam_l


def _s5_scan(lhs, prm_all, layer, nb):
    npair, rows, w = lhs.shape
    ball, toep, call, lam_l = prm_all
    spec_w = pl.BlockSpec((1, 1, w, w), lambda q: (layer, q, 0, 0))
    return pl.pallas_call(
        functools.partial(_s5_kernel, nb=nb, ctx_chunks=CTX_LEN // S5_CHUNK),
        grid=(npair,),
        in_specs=[pl.BlockSpec((1, rows, w), lambda q: (q, 0, 0)), spec_w, spec_w, spec_w,
                  pl.BlockSpec((1, 1, 1, w), lambda q: (layer, q, 0, 0))],
        out_specs=pl.BlockSpec((1, rows, w), lambda q: (q, 0, 0)),
        out_shape=jax.ShapeDtypeStruct((npair, rows, w), F32),
        scratch_shapes=[pltpu.VMEM((4, rows, LANES), F32)],
        compiler_params=_cp(("parallel",)),
        name="s5_scan",
    )(lhs, ball, toep, call, lam_l)


def _glu_kernel(y_ref, u_ref, d_ref, w_ref, b_ref, o_ref, yn):
    nb = u_ref.shape[0]
    cpb = S5_PACK_ROWS // S5_CHUNK
    ppt = LANES // S5_PIECE
    ntile = BRANCH_W // LANES
    for t in range(S5_CHUNK):
        for j in range(ntile):
            tile = jnp.concatenate([y_ref[j * ppt + k, :, t * S5_PIECE:(t + 1) * S5_PIECE]
                                    for k in range(ppt)], axis=-1)
            yn[j, :, pl.ds(t, cpb, stride=S5_CHUNK), :] = tile.reshape(nb, cpb, LANES)
    yd = jnp.concatenate([yn[j] for j in range(ntile)], axis=-1)
    y = d_ref[...] * u_ref[...].astype(F32).reshape(nb * S5_PACK_ROWS, BRANCH_W)
    y = y + yd.reshape(nb * S5_PACK_ROWS, BRANCH_W)
    cdf = 0.5 * (1.0 + jnp.tanh(math.sqrt(2.0 / math.pi) * (y + 0.044715 * (y * y * y))))
    g = y * cdf
    z = _dot(g.astype(BF16), w_ref[0]) + b_ref[...]
    o_ref[...] = (g * _sigmoid(z)).astype(o_ref.dtype).reshape(o_ref.shape)


def _s5_glu(y, p1_3d, d_skip, w_glu_all, layer, b_glu):
    nb, tlen, _ = p1_3d.shape
    bw = BRANCH_W
    rows = nb * (S5_PACK_ROWS // S5_CHUNK)
    const = lambda i: (0, 0)
    blk = pl.BlockSpec((nb, S5_PACK_ROWS, bw), lambda i: (0, i, 0))
    return pl.pallas_call(
        _glu_kernel,
        grid=(tlen // S5_PACK_ROWS,),
        in_specs=[pl.BlockSpec((S5_NPAIR, rows, bw), lambda i: (0, i, 0)), blk,
                  pl.BlockSpec((1, bw), const), pl.BlockSpec((1, bw, bw), lambda i: (layer, 0, 0)),
                  pl.BlockSpec((1, bw), const)],
        out_specs=blk,
        out_shape=jax.ShapeDtypeStruct((nb, tlen, bw), BF16),
        scratch_shapes=[pltpu.VMEM((bw // LANES, nb, S5_PACK_ROWS, LANES), F32)],
        compiler_params=_cp(("parallel",)),
        name="s5_glu",
    )(y, p1_3d, d_skip.reshape(1, bw), w_glu_all, b_glu.reshape(1, bw))


def _lane_masks(rows):
    lane = lax.broadcasted_iota(jnp.int32, (rows, LANES), 1)
    return lane < HEAD_DIM, (lane % HEAD_DIM) < (HEAD_DIM // 2)


def _head_norm(x, gain, lo):
    ss = x * x
    s_lo = jnp.sum(jnp.where(lo, ss, 0.0), axis=-1, keepdims=True)
    s_hi = jnp.sum(jnp.where(lo, 0.0, ss), axis=-1, keepdims=True)
    ms = jnp.where(lo, s_lo, s_hi) * (1.0 / HEAD_DIM)
    return x * lax.rsqrt(ms + EPS) * gain


def _rope(x, cos, sin_signed, first_half):
    partner = jnp.where(first_half, pltpu.roll(x, LANES - HEAD_DIM // 2, 1),
                        pltpu.roll(x, HEAD_DIM // 2, 1))
    return x * cos + partner * sin_signed


def _prep_slab(x, gain, cos, sin_signed, scale):
    rows = x.shape[0]
    lo, first_half = _lane_masks(rows)
    y = _head_norm(x.astype(F32), gain, lo)
    if cos is not None:
        y = _rope(y, cos, sin_signed, first_half)
    if scale is not None:
        y = y * scale
    return y.astype(BF16)


def _prep_keys(k_ref, kn_ref, gain, cos_ref, sin_ref, nslab):
    tlen = k_ref.shape[1]
    for r0 in range(0, tlen, SEG_ROWS):
        rows = slice(r0, r0 + SEG_ROWS)
        for s in range(nslab):
            cols = slice(s * LANES, (s + 1) * LANES)
            cos = None if cos_ref is None else cos_ref[rows, :]
            sin = None if sin_ref is None else sin_ref[rows, :]
            kn_ref[rows, cols] = _prep_slab(k_ref[0, rows, cols], gain, cos, sin, None)


def _prep_values(v_ref, va_ref, nslab):
    tlen = v_ref.shape[1]
    for r0 in range(0, tlen, SEG_ROWS):
        rows = slice(r0, r0 + SEG_ROWS)
        lo, _ = _lane_masks(SEG_ROWS)
        for s in range(nslab):
            cols = slice(s * LANES, (s + 1) * LANES)
            v = v_ref[0, rows, cols]
            one = jnp.ones_like(v)
            va_ref[0, rows, cols] = jnp.where(lo, v, one)
            va_ref[1, rows, cols] = jnp.where(lo, one, v)


def _split_heads(q):
    lo, _ = _lane_masks(q.shape[0])
    zero = jnp.zeros_like(q)
    return jnp.where(lo, q, zero), jnp.where(lo, zero, q), lo


def _join_heads(o_lo, o_hi, extra_lo, extra_hi, lo):
    num = jnp.where(lo, o_lo, o_hi)
    den = pltpu.roll(jnp.where(lo, o_hi, o_lo), HEAD_DIM, 1)
    if extra_lo is not None:
        den = den + jnp.where(lo, extra_lo, extra_hi)
    return num / den


def _win_kernel(sink_ref, q_ref, k_ref, v_ref, cos_ref, sin_ref, qg_ref, kg_ref, o_ref,
                kn_ref, va_ref, qs_ref, e_ref):
    qi = pl.program_id(1)
    tq = q_ref.shape[1]
    tlen = k_ref.shape[1]
    span = tq + 2 * WINDOW
    nhead = WIN_HEADS

    @pl.when(qi == 0)
    def _():
        _prep_keys(k_ref, kn_ref, kg_ref[...], cos_ref, sin_ref, 1)
        _prep_values(v_ref, va_ref, 1)

    qrows = pl.ds(pl.multiple_of(qi * tq, tq), tq)
    cos_q = cos_ref[qrows, :]
    sin_q = sin_ref[qrows, :]

    def attend(local):
        for s in range(nhead // 2):
            cols = slice(s * LANES, (s + 1) * LANES)
            qn = _prep_slab(q_ref[0, :, cols], qg_ref[...], cos_q, sin_q, SCALE * LOG2E)
            q_lo, q_hi, _ = _split_heads(qn)
            qs_ref[(2 * s) * tq:(2 * s + 1) * tq, :] = q_lo
            qs_ref[(2 * s + 1) * tq:(2 * s + 2) * tq, :] = q_hi
        qs = qs_ref[...]
        s_ctx = _dot_t(qs, kn_ref[0:CTX_LEN, :])
        m = jnp.max(s_ctx, axis=-1, keepdims=True)
        if local:
            start = jnp.clip(qi * tq - WINDOW, CTX_LEN, tlen - span)
            krows = pl.ds(pl.multiple_of(start, LANES), span)
            qtok = qi * tq + lax.broadcasted_iota(jnp.int32, (tq, span), 0)
            ktok = start + lax.broadcasted_iota(jnp.int32, (tq, span), 1)
            valid = (jnp.abs(qtok - ktok) <= WINDOW)[None]
            s_loc = _dot_t(qs, kn_ref[krows, :]).reshape(nhead, tq, span)
            s_loc = jnp.where(valid, s_loc, NEG_INF).reshape(nhead * tq, span)
            m = jnp.maximum(m, jnp.max(s_loc, axis=-1, keepdims=True))
        sinks = [sink_ref[(hb // 2) + (hb % 2) * (nhead // 2)] * LOG2E for hb in range(nhead)]
        m = jnp.concatenate([jnp.maximum(m[hb * tq:(hb + 1) * tq], sinks[hb]) for hb in range(nhead)], axis=0)
        e_ref[:, 0:CTX_LEN] = jnp.exp2(s_ctx - m).astype(BF16)
        if local:
            e_ref[:, CTX_LEN:CTX_LEN + span] = jnp.exp2(s_loc - m).astype(BF16)
        lo, _ = _lane_masks(tq)
        for s in range(nhead // 2):
            outs, extra = [], []
            for half in range(2):
                hb = 2 * s + half
                rows = slice(hb * tq, (hb + 1) * tq)
                o = _dot(e_ref[rows, 0:CTX_LEN], va_ref[half, 0:CTX_LEN, :])
                if local:
                    o = o + _dot(e_ref[rows, CTX_LEN:CTX_LEN + span], va_ref[half, krows, :])
                outs.append(o)
                extra.append(jnp.exp2(sinks[hb] - m[rows]))
            o_ref[0, :, s * LANES:(s + 1) * LANES] = _join_heads(
                outs[0], outs[1], extra[0], extra[1], lo).astype(o_ref.dtype)

    @pl.when(qi < CTX_LEN // tq)
    def _():
        attend(False)

    @pl.when(qi >= CTX_LEN // tq)
    def _():
        attend(True)


def _win_attn(p1, sink, cos_t, sin_t, qg, kg, tq=128):
    bsz, tlen, _ = p1.shape
    kcol = 8 * BRANCH_W // LANES
    return pl.pallas_call(
        _win_kernel,
        grid=(bsz, tlen // tq),
        in_specs=[pl.BlockSpec(memory_space=pltpu.SMEM),
                  pl.BlockSpec((1, tq, BRANCH_W), lambda b, i: (b, i, 1)),
                  pl.BlockSpec((1, tlen, LANES), lambda b, i: (b, 0, kcol)),
                  pl.BlockSpec((1, tlen, LANES), lambda b, i: (b, 0, kcol + 1)),
                  pl.BlockSpec((tlen, LANES), lambda b, i: (0, 0)),
                  pl.BlockSpec((tlen, LANES), lambda b, i: (0, 0)),
                  pl.BlockSpec((1, LANES), lambda b, i: (0, 0)),
                  pl.BlockSpec((1, LANES), lambda b, i: (0, 0))],
        out_specs=pl.BlockSpec((1, tq, BRANCH_W), lambda b, i: (b, i, 0)),
        out_shape=jax.ShapeDtypeStruct((bsz, tlen, BRANCH_W), BF16),
        scratch_shapes=[pltpu.VMEM((tlen, LANES), BF16), pltpu.VMEM((2, tlen, LANES), BF16),
                        pltpu.VMEM((WIN_HEADS * tq, LANES), BF16),
                        pltpu.VMEM((WIN_HEADS * tq, CTX_LEN + tq + 2 * WINDOW), BF16)],
        compiler_params=_cp(("parallel", "arbitrary")),
        name="win_attn",
    )(sink, p1, p1, p1, cos_t, sin_t, qg, kg)


def _diff_kernel(lam_ref, q_ref, k_ref, v_ref, cos_ref, sin_ref, qg_ref, kg_ref, sg_ref, o_ref, kn_ref):
    qi = pl.program_id(1)
    tq = q_ref.shape[1]
    tlen = k_ref.shape[1]
    lam = lam_ref[0]
    out_scale = lam_ref[1]

    @pl.when(qi == 0)
    def _():
        _prep_keys(k_ref, kn_ref, kg_ref[...], cos_ref, sin_ref, DIFF_HEADS)

    qrows = pl.ds(pl.multiple_of(qi * tq, tq), tq)
    cos_q = cos_ref[qrows, :]
    sin_q = sin_ref[qrows, :]

    def attend(nk):
        for h in range(DIFF_HEADS):
            cols = slice(h * LANES, (h + 1) * LANES)
            qn = _prep_slab(q_ref[0, :, cols], qg_ref[...], cos_q, sin_q, SCALE * LOG2E)
            q1, q2, _ = _split_heads(qn)
            keys = kn_ref[0:nk, cols]
            vals = v_ref[0, 0:nk, cols]
            s1 = _dot_t(q1, keys)
            s2 = _dot_t(q2, keys)
            e1 = jnp.exp2(s1 - jnp.max(s1, axis=-1, keepdims=True))
            e2 = jnp.exp2(s2 - jnp.max(s2, axis=-1, keepdims=True))
            r1 = 1.0 / _rowsum(e1)
            r2 = lam / _rowsum(e2)
            o = _dot(e1.astype(BF16), vals) * r1 - _dot(e2.astype(BF16), vals) * r2
            o = o * lax.rsqrt(jnp.mean(o * o, axis=-1, keepdims=True) + EPS) * sg_ref[...]
            o_ref[0, :, cols] = (o * out_scale).astype(o_ref.dtype)

    @pl.when(qi < CTX_LEN // tq)
    def _():
        attend(CTX_LEN)

    @pl.when(qi >= CTX_LEN // tq)
    def _():
        attend(tlen)


def _diff_attn(p1, lam_vec, cos_t, sin_t, qg, kg, sg, tq=256):
    bsz, tlen, _ = p1.shape
    full = lambda c: pl.BlockSpec((1, tlen, BRANCH_W), functools.partial(lambda b, i, c: (b, 0, c), c=c))
    const = lambda b, i: (0, 0)
    return pl.pallas_call(
        _diff_kernel,
        grid=(bsz, tlen // tq),
        in_specs=[pl.BlockSpec(memory_space=pltpu.SMEM),
                  pl.BlockSpec((1, tq, BRANCH_W), lambda b, i: (b, i, 2)),
                  full(3), full(4),
                  pl.BlockSpec((tlen, LANES), const), pl.BlockSpec((tlen, LANES), const),
                  pl.BlockSpec((1, LANES), const), pl.BlockSpec((1, LANES), const),
                  pl.BlockSpec((1, LANES), const)],
        out_specs=pl.BlockSpec((1, tq, BRANCH_W), lambda b, i: (b, i, 0)),
        out_shape=jax.ShapeDtypeStruct((bsz, tlen, BRANCH_W), BF16),
        scratch_shapes=[pltpu.VMEM((tlen, BRANCH_W), BF16)],
        compiler_params=_cp(("parallel", "arbitrary")),
        name="diff_attn",
    )(lam_vec, p1, p1, p1, cos_t, sin_t, qg, kg, sg)


def _na_kernel(q_ref, k_ref, v_ref, bias_ref, qg_ref, kg_ref, o_ref, kn_ref, va_ref, qs_ref, sc_ref, e_ref,
               *, grid_rows):
    qi = pl.program_id(1)
    span = NA_WIN_R * GRID_W
    nslab = NA_HEADS // 2

    @pl.when(qi == 0)
    def _():
        _prep_keys(k_ref, kn_ref, kg_ref[...], None, None, nslab)
        _prep_values(v_ref, va_ref, nslab)

    def attend(qrows, nq, krows):
        for s in range(nslab):
            cols = slice(s * LANES, (s + 1) * LANES)
            qn = _prep_slab(q_ref[0, qrows, cols], qg_ref[...], None, None, SCALE * LOG2E)
            q_lo, q_hi, _ = _split_heads(qn)
            qs_ref[(2 * s) * nq:(2 * s + 1) * nq, :] = q_lo
            qs_ref[(2 * s + 1) * nq:(2 * s + 2) * nq, :] = q_hi
        ncol = CTX_LEN if krows is None else CTX_LEN + span
        for s in range(nslab):
            cols = slice(s * LANES, (s + 1) * LANES)
            rows = slice(2 * s * nq, (2 * s + 2) * nq)
            sc_ref[rows, 0:CTX_LEN] = _dot_t(qs_ref[rows, :], kn_ref[0:CTX_LEN, cols])
            if krows is not None:
                bias = bias_ref[0, 2 * s:2 * s + 2].reshape(2 * nq, span)
                sc_ref[rows, CTX_LEN:ncol] = _dot_t(qs_ref[rows, :], kn_ref[krows, cols]) + bias
        nrow = NA_HEADS * nq
        sc = sc_ref[0:nrow, 0:ncol]
        e_ref[0:nrow, 0:ncol] = jnp.exp2(sc - jnp.max(sc, axis=-1, keepdims=True)).astype(BF16)
        lo, _ = _lane_masks(nq)
        for s in range(nslab):
            cols = slice(s * LANES, (s + 1) * LANES)
            outs = []
            for half in range(2):
                rows = slice((2 * s + half) * nq, (2 * s + half + 1) * nq)
                o = _dot(e_ref[rows, 0:CTX_LEN], va_ref[half, 0:CTX_LEN, cols])
                if krows is not None:
                    o = o + _dot(e_ref[rows, CTX_LEN:ncol], va_ref[half, krows, cols])
                outs.append(o)
            o_ref[0, qrows, cols] = _join_heads(outs[0], outs[1], None, None, lo).astype(o_ref.dtype)

    @pl.when(qi == 0)
    def _():
        attend(slice(0, CTX_LEN), CTX_LEN, None)

    @pl.when(qi > 0)
    def _():
        r = qi - 1
        qrows = pl.ds(pl.multiple_of(CTX_LEN + r * GRID_W, GRID_W), GRID_W)
        k0 = jnp.clip(r - NA_WIN_R // 2, 0, grid_rows - NA_WIN_R)
        krows = pl.ds(pl.multiple_of(CTX_LEN + k0 * GRID_W, GRID_W), span)
        attend(qrows, GRID_W, krows)


def _na_bias(rpb_all, rows):
    win_r = NA_WIN_R
    half = win_r // 2
    r = jnp.concatenate([jnp.arange(half + 1), jnp.arange(rows - half + 1, rows)])
    row_idx = jnp.clip(r - half, 0, rows - win_r)[:, None] + jnp.arange(win_r)[None, :]
    r_off = row_idx - r[:, None] + (NA_WIN_R - 1)
    col = jnp.arange(GRID_W)
    c_off = jnp.clip(col[None, :] - col[:, None] + (NA_WIN_C - 1), 0, 2 * NA_WIN_C - 2)
    onehot = (c_off[None] == jnp.arange(2 * NA_WIN_C - 1)[:, None, None]).astype(F32)
    rows_sel = rpb_all.astype(F32)[:, :, r_off] * LOG2E
    bias = jnp.einsum('lhpwk,kqc->lphqwc', rows_sel, onehot, precision=lax.Precision.HIGHEST)
    col_start = jnp.clip(col - NA_WIN_C // 2, 0, GRID_W - NA_WIN_C)
    col_ok = (col[None, :] >= col_start[:, None]) & (col[None, :] < col_start[:, None] + NA_WIN_C)
    bias = jnp.where(col_ok[None, None, None, :, None, :], bias, NEG_INF)
    depth = rpb_all.shape[0]
    return bias.reshape(depth, r.shape[0], NA_HEADS, GRID_W, win_r * GRID_W)


def _na_attn(p1, bias_all, layer, qg, kg):
    bsz, tlen, _ = p1.shape
    rows = (tlen - CTX_LEN) // GRID_W
    half = NA_WIN_R // 2
    full = lambda c: pl.BlockSpec((1, tlen, BRANCH_W), functools.partial(lambda b, i, c: (b, 0, c), c=c))
    const = lambda b, i: (0, 0)

    def bias_idx(b, i):
        r = jnp.maximum(i - 1, 0)
        return (layer, jnp.minimum(r, half) + jnp.maximum(r - (rows - half), 0), 0, 0, 0)

    return pl.pallas_call(
        functools.partial(_na_kernel_wrap, grid_rows=rows),
        grid=(bsz, rows + 1),
        in_specs=[full(5), full(6), full(7),
                  pl.BlockSpec((1, 1, NA_HEADS, GRID_W, NA_WIN_R * GRID_W), bias_idx),
                  pl.BlockSpec((1, LANES), const), pl.BlockSpec((1, LANES), const)],
        out_specs=pl.BlockSpec((1, tlen, BRANCH_W), lambda b, i: (b, 0, 0)),
        out_shape=jax.ShapeDtypeStruct((bsz, tlen, BRANCH_W), BF16),
        scratch_shapes=[pltpu.VMEM((tlen, BRANCH_W), BF16), pltpu.VMEM((2, tlen, BRANCH_W), BF16),
                        pltpu.VMEM((NA_HEADS * CTX_LEN, LANES), BF16),
                        pltpu.VMEM((NA_HEADS * CTX_LEN, CTX_LEN + NA_WIN_R * GRID_W), F32),
                        pltpu.VMEM((NA_HEADS * CTX_LEN, CTX_LEN + NA_WIN_R * GRID_W), BF16)],
        compiler_params=_cp(("parallel", "arbitrary")),
        name="na_attn",
    )(p1, p1, p1, bias_all, qg, kg)


def _na_kernel_wrap(q_ref, k_ref, v_ref, bias_ref, *rest, grid_rows):
    _na_kernel(q_ref, k_ref, v_ref, bias_ref.at[0], *rest, grid_rows=grid_rows)


def _row_copy(src_hbm, row, dst, r, sem):
    return pltpu.make_async_copy(src_hbm.at[pl.ds(row, 1), :], dst.at[pl.ds(r, 1), :], sem)


def _gather_rows(idx_ref, base, src_hbm, dst, sem, nrows):
    for r in range(nrows):
        _row_copy(src_hbm, idx_ref[base + r], dst, r, sem).start()


def _gather_wait(src_hbm, dst, sem, nrows):
    pltpu.make_async_copy(src_hbm.at[pl.ds(0, nrows), :], dst, sem).wait()


def _expert_kernel(te_ref, rt_ref, na_ref, h_hbm, rw_ref, w1_ref, w3_ref, w2_ref, o_ref, buf, sem):
    i = pl.program_id(0)
    tm = buf.shape[1]
    slot = i % 2
    nact = jnp.maximum(na_ref[0], 1)

    @pl.when(i == 0)
    def _():
        _gather_rows(rt_ref, 0, h_hbm, buf.at[0], sem.at[0], tm)

    @pl.when(i < nact)
    def _():
        _gather_wait(h_hbm, buf.at[slot], sem.at[slot], tm)
        nxt = jnp.minimum(i + 1, pl.num_programs(0) - 1)
        _gather_rows(rt_ref, nxt * tm, h_hbm, buf.at[1 - slot], sem.at[1 - slot], tm)
        x = buf[slot].astype(BF16)
        a = _dot(x, w1_ref[0, 0])
        mid = (a * jax.nn.sigmoid(a)) * _dot(x, w3_ref[0, 0])
        o_ref[...] = _dot(mid.astype(BF16), w2_ref[0, 0]) * rw_ref[...]

    @pl.when(i == nact - 1)
    def _():
        _gather_wait(h_hbm, buf.at[1 - slot], sem.at[1 - slot], tm)

    @pl.when(i >= nact)
    def _():
        o_ref[...] = jnp.zeros_like(o_ref)


def _moe_experts(h, tile_expert, row_token, n_active, row_weight, w1_all, w3_all, w2_all, layer, tm=MOE_TM):
    n, d = h.shape
    rmax = row_token.shape[0]
    ff = w1_all.shape[3]
    grid_spec = pltpu.PrefetchScalarGridSpec(
        num_scalar_prefetch=3,
        grid=(rmax // tm,),
        in_specs=[pl.BlockSpec(memory_space=pl.ANY),
                  pl.BlockSpec((tm, 1), lambda i, te, rt, na: (i, 0)),
                  pl.BlockSpec((1, 1, d, ff), lambda i, te, rt, na: (layer, te[i], 0, 0)),
                  pl.BlockSpec((1, 1, d, ff), lambda i, te, rt, na: (layer, te[i], 0, 0)),
                  pl.BlockSpec((1, 1, ff, d), lambda i, te, rt, na: (layer, te[i], 0, 0))],
        out_specs=pl.BlockSpec((tm, d), lambda i, te, rt, na: (i, 0)),
        scratch_shapes=[pltpu.VMEM((2, tm, d), F32), pltpu.SemaphoreType.DMA((2,))])
    return pl.pallas_call(
        _expert_kernel,
        grid_spec=grid_spec,
        out_shape=jax.ShapeDtypeStruct((rmax, d), F32),
        compiler_params=_cp(("arbitrary",)),
        name="moe_experts",
    )(tile_expert, row_token, n_active, h, row_weight, w1_all, w3_all, w2_all)


def _combine_kernel(d0_ref, d1_ref, ys_hbm, x_ref, mod_ref, *rest, blocks_per_batch, next_norm):
    if next_norm:
        g_ref, sh_ref, sc_ref, o_ref, h_ref, buf0, buf1, sem = rest
    else:
        o_ref, buf0, buf1, sem = rest
    i = pl.program_id(0)
    n = pl.num_programs(0)
    tm = x_ref.shape[0]
    slot = i % 2

    def issue(tile, s):
        _gather_rows(d0_ref, tile * tm, ys_hbm, buf0.at[s], sem.at[0, s], tm)
        _gather_rows(d1_ref, tile * tm, ys_hbm, buf1.at[s], sem.at[1, s], tm)

    @pl.when(i == 0)
    def _():
        issue(0, 0)

    @pl.when(i + 1 < n)
    def _():
        issue(i + 1, 1 - slot)

    _gather_wait(ys_hbm, buf0.at[slot], sem.at[0, slot], tm)
    _gather_wait(ys_hbm, buf1.at[slot], sem.at[1, slot], tm)
    row = _seg_mod_row(i, blocks_per_batch)
    x_new = x_ref[...] + mod_ref[row] * (buf0[slot] + buf1[slot])
    o_ref[...] = x_new
    if next_norm:
        h_ref[...] = _mod_norm(x_new, g_ref[...], sh_ref[row], sc_ref[row]).astype(h_ref.dtype)


def _moe_combine(ys, dest0, dest1, x, modtab, k_mod, blocks_per_batch, next_norm=None):
    n, d = x.shape
    tm = SEG_ROWS
    nrow = modtab.shape[0]
    row_blk = pl.BlockSpec((tm, d), lambda i, a, b: (i, 0))
    mod_spec = lambda k: pl.BlockSpec((nrow, 1, d), lambda i, a, b: (0, 0, k))
    in_specs = [pl.BlockSpec(memory_space=pl.ANY), row_blk, mod_spec(k_mod)]
    args = [dest0, dest1, ys, x, modtab]
    out_specs = row_blk
    out_shape = jax.ShapeDtypeStruct((n, d), F32)
    if next_norm is not None:
        gain, modtab_next, k_shift, k_scale = next_norm
        in_specs += [pl.BlockSpec((1, d), lambda i, a, b: (0, 0)), mod_spec(k_shift), mod_spec(k_scale)]
        args += [gain.reshape(1, d), modtab_next, modtab_next]
        out_specs = [row_blk, row_blk]
        out_shape = [out_shape, jax.ShapeDtypeStruct((n, d), BF16)]
    grid_spec = pltpu.PrefetchScalarGridSpec(
        num_scalar_prefetch=2,
        grid=(n // tm,),
        in_specs=in_specs,
        out_specs=out_specs,
        scratch_shapes=[pltpu.VMEM((2, tm, d), F32), pltpu.VMEM((2, tm, d), F32),
                        pltpu.SemaphoreType.DMA((2, 2))])
    return pl.pallas_call(
        functools.partial(_combine_kernel, blocks_per_batch=blocks_per_batch, next_norm=next_norm is not None),
        grid_spec=grid_spec,
        out_shape=out_shape,
        compiler_params=_cp(("arbitrary",)),
        name="moe_combine",
    )(*args)


def _route(logits, tm=MOE_TM):
    n = logits.shape[0]
    g_prob = jax.nn.softmax(logits[:, :N_GROUPS], axis=-1)
    g_idx = jnp.argmax(g_prob, axis=-1).astype(jnp.int32)
    g_w = jnp.max(g_prob, axis=-1)
    e_logits = logits[:, N_GROUPS:N_GROUPS + N_EXPERTS].reshape(n, N_GROUPS, EXPERTS_PER_GROUP)
    in_group = jnp.arange(N_GROUPS, dtype=jnp.int32)[None, :, None] == g_idx[:, None, None]
    e_in = jnp.sum(jnp.where(in_group, e_logits, 0.0), axis=1)
    slot = jnp.arange(EXPERTS_PER_GROUP, dtype=jnp.int32)[None, :]
    i1 = jnp.argmax(e_in, axis=-1).astype(jnp.int32)
    v1 = jnp.max(e_in, axis=-1)
    rest = jnp.where(slot == i1[:, None], -jnp.inf, e_in)
    i2 = jnp.argmax(rest, axis=-1).astype(jnp.int32)
    v2 = jnp.max(rest, axis=-1)
    w_sel = jax.nn.softmax(jnp.stack([v1, v2], axis=-1), axis=-1) * g_w[:, None]
    expert = g_idx[:, None] * EXPERTS_PER_GROUP + jnp.stack([i1, i2], axis=-1)

    e_flat = expert.reshape(-1)
    w_flat = w_sel.reshape(-1)
    ids = jnp.arange(N_EXPERTS, dtype=jnp.int32)
    onehot = (e_flat[:, None] == ids[None, :]).astype(jnp.int32)
    csum = jnp.cumsum(onehot, axis=0)
    rank = jnp.sum(onehot * csum, axis=1) - 1
    counts = csum[-1]
    padded = ((counts + tm - 1) // tm) * tm
    pend = jnp.cumsum(padded)
    pstart = pend - padded
    ustart = jnp.cumsum(counts) - counts
    dest = (jnp.sum(onehot * pstart[None, :], axis=1) + rank).astype(jnp.int32)

    rmax = 2 * n + N_EXPERTS * tm
    total = pend[-1]
    tile_start = jnp.arange(rmax // tm, dtype=jnp.int32) * tm
    tile_expert = jnp.minimum(jnp.sum((tile_start[:, None] >= pend[None, :]).astype(jnp.int32), axis=1),
                              N_EXPERTS - 1)
    order = jnp.argsort(e_flat, stable=True).astype(jnp.int32)
    r = jnp.arange(rmax, dtype=jnp.int32)
    e_r = jnp.minimum(jnp.sum((r[:, None] >= pend[None, :]).astype(jnp.int32), axis=1), N_EXPERTS - 1)
    oh_r = (e_r[:, None] == ids[None, :]).astype(jnp.int32)
    off = r - jnp.sum(oh_r * pstart[None, :], axis=1)
    valid = (r < total) & (off < jnp.sum(oh_r * counts[None, :], axis=1))
    src = order[jnp.clip(jnp.sum(oh_r * ustart[None, :], axis=1) + off, 0, 2 * n - 1)]
    row_token = jnp.where(valid, src // 2, 0).astype(jnp.int32)
    row_weight = jnp.where(valid, w_flat[src], 0.0)
    n_active = (total // tm).astype(jnp.int32)
    last_expert = tile_expert[jnp.maximum(n_active - 1, 0)]
    tile_expert = jnp.where(tile_start < total, tile_expert, last_expert).astype(jnp.int32)
    dest2 = dest.reshape(n, 2)
    return (tile_expert, row_token, n_active.reshape(1), row_weight.reshape(rmax, 1),
            dest2[:, 0], dest2[:, 1])


def _rope_tables(n_lat):
    t = jnp.arange(n_lat, dtype=jnp.int32)
    row = (t // GRID_W).astype(F32)
    col = (t % GRID_W).astype(F32)
    per_axis = HEAD_DIM // 4
    inv_freq = ROPE_BASE ** (-jnp.arange(per_axis, dtype=F32) / per_axis)
    ang = jnp.concatenate([row[:, None] * inv_freq, col[:, None] * inv_freq], axis=-1)
    cos = jnp.concatenate([jnp.ones((CTX_LEN, HEAD_DIM // 2), F32), jnp.cos(ang)], axis=0)
    sin = jnp.concatenate([jnp.zeros((CTX_LEN, HEAD_DIM // 2), F32), jnp.sin(ang)], axis=0)
    cos_t = jnp.tile(cos, (1, 4))
    sin_t = jnp.tile(jnp.concatenate([-sin, sin], axis=-1), (1, 2))
    return cos_t, sin_t


def _win_head_perm():
    cols = []
    for s in range(WIN_HEADS // 2):
        for half in range(2):
            h = s + half * (WIN_HEADS // 2)
            cols.extend(range(h * HEAD_DIM, (h + 1) * HEAD_DIM))
    return cols


def _p1_weight(w_in):
    bw = BRANCH_W
    kvw = WIN_KV_HEADS * HEAD_DIM
    hd = HEAD_DIM
    cols = lambda a, b: w_in[:, :, a:b].astype(BF16)
    q_b = [cols(bw + h * hd, bw + (h + 1) * hd) for s in range(WIN_HEADS // 2) for h in (s, s + WIN_HEADS // 2)]
    kv0 = 2 * bw
    rest0 = kv0 + 2 * kvw
    pad = jnp.zeros(w_in.shape[:2] + (bw - 2 * kvw,), BF16)
    p1 = jnp.concatenate([cols(0, bw)] + q_b + [cols(rest0, rest0 + 6 * bw), cols(kv0, rest0), pad], axis=2)
    return p1, cols(rest0 + 6 * bw, w_in.shape[2])


def kernel(x, c, ctx, c_ctx, w_ada, b_ada, norm_mix, norm_ffn, w_in, s5_a_re, s5_a_im, s5_log_step, s5_b_re, s5_b_im, s5_c_re, s5_c_im, s5_d, s5_w_glu, s5_b_glu, win_qn, win_kn, win_sink, diff_qn, diff_kn, diff_lambda, diff_subln, na_qn, na_kn, na_rpb, w_branch, w_out, moe_w_group, moe_b_group, moe_w_expert, moe_b_expert, moe_w1, moe_w3, moe_w2):
    bsz, n_lat, d = x.shape
    tlen = CTX_LEN + n_lat
    ntok = bsz * tlen
    bpb = tlen // SEG_ROWS
    depth = w_ada.shape[0]

    cond = jnp.zeros((ADA_ROWS, d), F32).at[:bsz].set(c).at[bsz].set(c_ctx)
    mod = _ada_mod(cond, w_ada, b_ada)
    mod_ctx = jnp.broadcast_to(mod[:, bsz:bsz + 1], (depth, bsz, 6 * d))
    modtab = jnp.stack([mod_ctx, mod[:, :bsz]], axis=2).reshape(depth, 2 * bsz, 1, 6 * d)

    w_p1, w_gate = _p1_weight(w_in)
    wb_all = w_branch.astype(BF16)
    wb_win = jnp.concatenate([wb_all[:, 1:2, h * HEAD_DIM:(h + 1) * HEAD_DIM]
                              for s in range(WIN_HEADS // 2) for h in (s, s + WIN_HEADS // 2)], axis=2)
    wb_all = jnp.concatenate([wb_all[:, :1], wb_win, wb_all[:, 2:]], axis=1)
    w_out_bf = w_out.astype(BF16)
    w_glu_bf = s5_w_glu.astype(BF16)
    w1_bf, w3_bf, w2_bf = moe_w1.astype(BF16), moe_w3.astype(BF16), moe_w2.astype(BF16)
    w_r = jnp.zeros((depth, d, LANES), F32).at[:, :, :N_GROUPS].set(moe_w_group)
    w_r = w_r.at[:, :, N_GROUPS:N_GROUPS + N_EXPERTS].set(moe_w_expert)
    rb = jnp.zeros((depth, 1, LANES), F32).at[:, 0, :N_GROUPS].set(moe_b_group)
    rb = rb.at[:, 0, N_GROUPS:N_GROUPS + N_EXPERTS].set(moe_b_expert)
    wr_hi, wr_lo = _split_bf16(w_r)
    na_bias = _na_bias(na_rpb, n_lat // GRID_W)
    s5_prm = jax.vmap(_s5_params)(s5_a_re, s5_a_im, s5_log_step, s5_b_re, s5_b_im, s5_c_re, s5_c_im)

    cos_t, sin_t = _rope_tables(n_lat)
    tile2 = lambda g: jnp.tile(g.astype(F32), 2).reshape(1, LANES)
    xs = jnp.concatenate([ctx, x], axis=1)

    for l in range(depth):
        lam_init = 0.8 - 0.6 * math.exp(-0.3 * l)
        mt = modtab[l]

        if l == 0:
            h = _norm(xs, norm_mix[l], mt, 0, 1).reshape(ntok, d)
        p1 = _matmul(h, w_p1, l, BF16, tn=P1_WIDTH // 3)
        gates = _matmul(h, w_gate, l, BF16, tn=1024)
        p1_3d = p1.reshape(bsz, tlen, P1_WIDTH)

        y_s5 = _s5_scan(_s5_pack(p1_3d), s5_prm, l, bsz)
        y_a = _s5_glu(y_s5, p1_3d, s5_d[l], w_glu_bf, l, s5_b_glu[l]).reshape(ntok, BRANCH_W)

        y_b = _win_attn(p1_3d, win_sink[l].astype(F32), cos_t, sin_t,
                        tile2(win_qn[l]), tile2(win_kn[l])).reshape(ntok, BRANCH_W)

        lp = diff_lambda[l].astype(F32)
        lam = jnp.exp(jnp.sum(lp[0] * lp[1])) - jnp.exp(jnp.sum(lp[2] * lp[3])) + lam_init
        lam_vec = jnp.stack([lam, jnp.asarray(1.0 - lam_init, F32)])
        y_c = _diff_attn(p1_3d, lam_vec, cos_t, sin_t, tile2(diff_qn[l]), tile2(diff_kn[l]),
                         diff_subln[l].astype(F32).reshape(1, LANES)).reshape(ntok, BRANCH_W)

        y_d = _na_attn(p1_3d, na_bias, l, tile2(na_qn[l]), tile2(na_kn[l])).reshape(ntok, BRANCH_W)

        merged = _merge((y_a, y_b, y_c, y_d), gates, wb_all, l)
        xs = _matmul_resid(merged, w_out_bf, l, xs.reshape(ntok, d), mt, 2, bpb, tn=1024)
        xs = xs.reshape(bsz, tlen, d)

        hf, logits = _norm_router(xs, norm_ffn[l], mt, 3, 4, wr_hi[l], wr_lo[l], rb[l])
        te, rt, na, rw, d0, d1 = _route(logits.reshape(ntok, LANES))
        ys = _moe_experts(hf.reshape(ntok, d), te, rt, na, rw, w1_bf, w3_bf, w2_bf, l)
        if l + 1 < depth:
            xs, h = _moe_combine(ys, d0, d1, xs.reshape(ntok, d), mt, 5, bpb,
                                 next_norm=(norm_mix[l + 1], modtab[l + 1], 0, 1))
        else:
            xs = _moe_combine(ys, d0, d1, xs.reshape(ntok, d), mt, 5, bpb)
        xs = xs.reshape(bsz, tlen, d)

    return xs[:, CTX_LEN:]
```

```python
import functools
import math

import jax
import jax.numpy as jnp
from jax import lax
from jax.experimental import pallas as pl
from jax.experimental.pallas import tpu as pltpu

F32 = jnp.float32
BF16 = jnp.bfloat16

D_MODEL = 2048
DEPTH = 4
GRID_W = 64
CTX_LEN = 256
HEAD_DIM = 64
BRANCH_W = 512
N_BRANCH = 4
S5_CH = 16
S5_GROUPS = BRANCH_W // S5_CH
S5_STATE = 64
WIN_HEADS = 8
WIN_KV_HEADS = 2
WINDOW = 128
DIFF_HEADS = 4
NA_HEADS = 8
NA_WIN_R = 8
NA_WIN_C = 16
N_GROUPS = 4
EXPERTS_PER_GROUP = 4
N_EXPERTS = 16
EXPERT_FF = D_MODEL // 2
ROPE_BASE = 100.0
EPS = 1e-6
NEG_INF = -1e30

LANES = 128
SEG_ROWS = 256
ADA_ROWS = 16
P1_WIDTH = 9 * BRANCH_W
MOE_TM = 256
VMEM_LIMIT = 52 * 1024 * 1024

S5_CHUNK = 16
S5_PACK_ROWS = 128
S5_NPAIR = S5_GROUPS // 2
S5_PIECE = 2 * S5_CH

LOG2E = 1.4426950408889634
SCALE = HEAD_DIM ** -0.5


def _cp(sem, vmem=VMEM_LIMIT):
    return pltpu.CompilerParams(dimension_semantics=sem, vmem_limit_bytes=vmem)


def _dot(a, b):
    return jnp.dot(a, b, preferred_element_type=F32)


def _dot_t(a, b):
    return lax.dot_general(a, b, (((1,), (1,)), ((), ())), preferred_element_type=F32)


def _split_bf16(x):
    hi = x.astype(BF16)
    lo = (x - hi.astype(F32)).astype(BF16)
    return hi, lo


def _sigmoid(x):
    return 0.5 * jnp.tanh(0.5 * x) + 0.5


def _rowsum(x):
    acc = x[:, :LANES]
    for j in range(1, x.shape[1] // LANES):
        acc = acc + x[:, j * LANES:(j + 1) * LANES]
    return jnp.sum(acc, axis=-1, keepdims=True)


def _ada_kernel(c_ref, w_ref, b_ref, o_ref):
    c = c_ref[...]
    a_hi, a_lo = _split_bf16(c * jax.nn.sigmoid(c))
    w_hi, w_lo = _split_bf16(w_ref[0])
    o_ref[0] = _dot(a_hi, w_hi) + _dot(a_lo, w_hi) + _dot(a_hi, w_lo) + b_ref[0]


def _ada_mod(cond, w_ada, b_ada):
    depth, d, n = w_ada.shape
    tn = 1024
    return pl.pallas_call(
        _ada_kernel,
        grid=(depth, n // tn),
        in_specs=[pl.BlockSpec((ADA_ROWS, d), lambda l, j: (0, 0)),
                  pl.BlockSpec((1, d, tn), lambda l, j: (l, 0, j)),
                  pl.BlockSpec((1, 1, tn), lambda l, j: (l, 0, j))],
        out_specs=pl.BlockSpec((1, ADA_ROWS, tn), lambda l, j: (l, 0, j)),
        out_shape=jax.ShapeDtypeStruct((depth, ADA_ROWS, n), F32),
        compiler_params=_cp(("arbitrary", "arbitrary")),
        name="ada_mod",
    )(cond, w_ada, b_ada.reshape(depth, 1, n))


def _mod_norm(x, gain, sh, sc):
    y = x * lax.rsqrt(jnp.mean(x * x, axis=-1, keepdims=True) + EPS) * gain
    return y * (1.0 + sc) + sh


def _norm_kernel(x_ref, g_ref, sh_ref, sc_ref, o_ref):
    o_ref[0] = _mod_norm(x_ref[0], g_ref[...], sh_ref[0], sc_ref[0]).astype(o_ref.dtype)


def _norm_router_kernel(x_ref, g_ref, sh_ref, sc_ref, whi_ref, wlo_ref, rb_ref, o_ref, lg_ref):
    h = _mod_norm(x_ref[0], g_ref[...], sh_ref[0], sc_ref[0])
    o_ref[0] = h
    h_hi, h_lo = _split_bf16(h)
    lg_ref[0] = (_dot(h_hi, whi_ref[...]) + _dot(h_lo, whi_ref[...]) + _dot(h_hi, wlo_ref[...])
                 + rb_ref[...])


def _norm_specs(d, k_shift, k_scale):
    def mod_spec(k):
        return pl.BlockSpec((1, 1, d), lambda b, t: (2 * b + jnp.minimum(t, 1), 0, k))
    return [pl.BlockSpec((1, SEG_ROWS, d), lambda b, t: (b, t, 0)),
            pl.BlockSpec((1, d), lambda b, t: (0, 0)),
            mod_spec(k_shift), mod_spec(k_scale)]


def _norm(xs, gain, modtab, k_shift, k_scale):
    bsz, tlen, d = xs.shape
    return pl.pallas_call(
        _norm_kernel,
        grid=(bsz, tlen // SEG_ROWS),
        in_specs=_norm_specs(d, k_shift, k_scale),
        out_specs=pl.BlockSpec((1, SEG_ROWS, d), lambda b, t: (b, t, 0)),
        out_shape=jax.ShapeDtypeStruct((bsz, tlen, d), BF16),
        compiler_params=_cp(("parallel", "parallel")),
        name="mod_norm",
    )(xs, gain.reshape(1, d), modtab, modtab)


def _norm_router(xs, gain, modtab, k_shift, k_scale, wr_hi, wr_lo, rb):
    bsz, tlen, d = xs.shape
    blk = pl.BlockSpec((1, SEG_ROWS, d), lambda b, t: (b, t, 0))
    const = lambda b, t: (0, 0)
    return pl.pallas_call(
        _norm_router_kernel,
        grid=(bsz, tlen // SEG_ROWS),
        in_specs=_norm_specs(d, k_shift, k_scale) + [
            pl.BlockSpec((d, LANES), const), pl.BlockSpec((d, LANES), const),
            pl.BlockSpec((1, LANES), const)],
        out_specs=[blk, pl.BlockSpec((1, SEG_ROWS, LANES), lambda b, t: (b, t, 0))],
        out_shape=[jax.ShapeDtypeStruct((bsz, tlen, d), F32),
                   jax.ShapeDtypeStruct((bsz, tlen, LANES), F32)],
        compiler_params=_cp(("parallel", "parallel")),
        name="mod_norm_router",
    )(xs, gain.reshape(1, d), modtab, modtab, wr_hi, wr_lo, rb)


def _mm_kernel(a_ref, w_ref, o_ref):
    o_ref[...] = _dot(a_ref[...], w_ref[0]).astype(o_ref.dtype)


def _matmul(a, w_all, layer, out_dtype, tm=1024, tn=512):
    m, k = a.shape
    n = w_all.shape[2]
    return pl.pallas_call(
        _mm_kernel,
        grid=(m // tm, n // tn),
        in_specs=[pl.BlockSpec((tm, k), lambda i, j: (i, 0)),
                  pl.BlockSpec((1, k, tn), lambda i, j: (layer, 0, j))],
        out_specs=pl.BlockSpec((tm, tn), lambda i, j: (i, j)),
        out_shape=jax.ShapeDtypeStruct((m, n), out_dtype),
        compiler_params=_cp(("parallel", "parallel")),
        name="matmul",
    )(a, w_all)


def _seg_mod_row(block, blocks_per_batch):
    b = block // blocks_per_batch
    return 2 * b + jnp.minimum(block % blocks_per_batch, 1)


def _mm_resid_kernel(a_ref, w_ref, x_ref, mod_ref, o_ref, *, blocks_per_batch):
    acc = _dot(a_ref[...], w_ref[0])
    sub = a_ref.shape[0] // SEG_ROWS
    for s in range(sub):
        row = _seg_mod_row(pl.program_id(0) * sub + s, blocks_per_batch)
        rows = slice(s * SEG_ROWS, (s + 1) * SEG_ROWS)
        o_ref[rows, :] = x_ref[rows, :] + mod_ref[row] * acc[rows, :]


def _matmul_resid(a, w_all, layer, x, modtab, k_mod, blocks_per_batch, tm=1024, tn=512):
    m, k = a.shape
    n = w_all.shape[2]
    nrow = modtab.shape[0]
    return pl.pallas_call(
        functools.partial(_mm_resid_kernel, blocks_per_batch=blocks_per_batch),
        grid=(m // tm, n // tn),
        in_specs=[pl.BlockSpec((tm, k), lambda i, j: (i, 0)),
                  pl.BlockSpec((1, k, tn), lambda i, j: (layer, 0, j)),
                  pl.BlockSpec((tm, tn), lambda i, j: (i, j)),
                  pl.BlockSpec((nrow, 1, tn), lambda i, j: (0, 0, k_mod * (n // tn) + j))],
        out_specs=pl.BlockSpec((tm, tn), lambda i, j: (i, j)),
        out_shape=jax.ShapeDtypeStruct((m, n), F32),
        compiler_params=_cp(("parallel", "parallel")),
        name="matmul_resid",
    )(a, w_all, x, modtab)


def _merge_kernel(ya_ref, yb_ref, yc_ref, yd_ref, ga_ref, gb_ref, gc_ref, gd_ref, wb_ref, o_ref):
    acc = None
    for i, (y_ref, g_ref) in enumerate(((ya_ref, ga_ref), (yb_ref, gb_ref),
                                        (yc_ref, gc_ref), (yd_ref, gd_ref))):
        term = _sigmoid(g_ref[...].astype(F32)) * _dot(y_ref[...], wb_ref[0, i])
        acc = term if acc is None else acc + term
    o_ref[...] = acc.astype(o_ref.dtype)


def _merge(ys, gates, wb_all, layer, tm=1024, tn=512):
    m, bw = ys[0].shape
    n = wb_all.shape[3]
    nj = n // tn
    y_spec = pl.BlockSpec((tm, bw), lambda i, j: (i, 0))
    g_specs = [pl.BlockSpec((tm, tn), functools.partial(lambda i, j, q: (i, q * nj + j), q=q))
               for q in range(N_BRANCH)]
    return pl.pallas_call(
        _merge_kernel,
        grid=(m // tm, nj),
        in_specs=[y_spec] * N_BRANCH + g_specs + [
            pl.BlockSpec((1, N_BRANCH, bw, tn), lambda i, j: (layer, 0, 0, j))],
        out_specs=pl.BlockSpec((tm, tn), lambda i, j: (i, j)),
        out_shape=jax.ShapeDtypeStruct((m, n), BF16),
        compiler_params=_cp(("parallel", "parallel")),
        name="branch_merge",
    )(*ys, gates, gates, gates, gates, wb_all)


def _s5_pack_kernel(x_ref, o_ref, xf, ob):
    nb = x_ref.shape[0]
    cpb = S5_PACK_ROWS // S5_CHUNK
    ppt = LANES // S5_PIECE
    for j in range(BRANCH_W // LANES):
        xf[j] = x_ref[:, :, j * LANES:(j + 1) * LANES].astype(F32)
    for t in range(S5_CHUNK):
        for j in range(BRANCH_W // LANES):
            rows = xf[j, :, pl.ds(t, cpb, stride=S5_CHUNK), :].reshape(nb * cpb, LANES)
            for k in range(ppt):
                ob[j * ppt + k, :, t * S5_PIECE:(t + 1) * S5_PIECE] = rows[:, k * S5_PIECE:(k + 1) * S5_PIECE]
    o_ref[...] = ob[...].astype(o_ref.dtype)


def _s5_pack(p1_3d):
    nb, tlen, _ = p1_3d.shape
    rows = nb * (S5_PACK_ROWS // S5_CHUNK)
    return pl.pallas_call(
        _s5_pack_kernel,
        grid=(tlen // S5_PACK_ROWS,),
        in_specs=[pl.BlockSpec((nb, S5_PACK_ROWS, BRANCH_W), lambda i: (0, i, 0))],
        out_specs=pl.BlockSpec((S5_NPAIR, rows, BRANCH_W), lambda i: (0, i, 0)),
        out_shape=jax.ShapeDtypeStruct((S5_NPAIR, (tlen // S5_PACK_ROWS) * rows, BRANCH_W), BF16),
        scratch_shapes=[pltpu.VMEM((BRANCH_W // LANES, nb, S5_PACK_ROWS, LANES), F32),
                        pltpu.VMEM((S5_NPAIR, rows, BRANCH_W), F32)],
        compiler_params=_cp(("parallel",)),
        name="s5_pack",
    )(p1_3d)


def _s5_kernel(u_ref, ball_ref, toep_ref, call_ref, lam_ref, y_ref, st, *, nb, ctx_chunks):
    u = u_ref[0]
    s_all = _dot(u, ball_ref[0, 0])
    for k in range(4):
        st[k] = s_all[:, k * LANES:(k + 1) * LANES]
    cpb = S5_PACK_ROWS // S5_CHUNK
    nchunk = u.shape[0] // nb
    lam = lam_ref[0, 0]
    lfr, lfi, lbr, lbi = [jnp.broadcast_to(lam[:, k * LANES:(k + 1) * LANES], (nb, LANES)) for k in range(4)]

    def chunk_rows(c):
        return pl.ds((c // cpb) * (cpb * nb) + c % cpb, nb, stride=cpb)

    def step(rows, k, xr, xi, lr, li):
        s_r = st[2 * k, rows, :]
        s_i = st[2 * k + 1, rows, :]
        st[2 * k, rows, :] = xr
        st[2 * k + 1, rows, :] = xi
        return lr * xr - li * xi + s_r, lr * xi + li * xr + s_i

    def body(k, carry):
        fr, fi, br, bi = carry
        fr, fi = step(chunk_rows(k), 0, fr, fi, lfr, lfi)
        cb = jnp.where(k < ctx_chunks, ctx_chunks - 1 - k, nchunk - 1 - (k - ctx_chunks))
        br, bi = step(chunk_rows(cb), 1, br, bi, lbr, lbi)
        return fr, fi, br, bi

    zero = jnp.zeros((nb, LANES), F32)
    lax.fori_loop(0, nchunk, body, (zero, zero, zero, zero))
    states = jnp.concatenate([st[k] for k in range(4)], axis=-1).astype(BF16)
    y_ref[0] = _dot(u, toep_ref[0, 0]) + _dot_t(states, call_ref[0, 0])


def _s5_params(a_re, a_im, log_step, b_re, b_im, c_re, c_im):
    L, P, CH, NP = S5_CHUNK, S5_STATE, S5_CH, S5_NPAIR
    lam = lax.complex(a_re, a_im)
    lam_dt = lam * jnp.exp(log_step)[..., None]
    lam_bar = jnp.exp(lam_dt)
    b_bar = ((lam_bar - 1.0) / lam)[..., None] * lax.complex(b_re, b_im)
    c_mat = lax.complex(c_re, c_im)
    t = jnp.arange(L, dtype=F32)
    mid = float(L // 2)

    def power(z, exps):
        return jnp.exp(lam_dt[z][None] * exps[:, None, None].astype(jnp.complex64))

    def row_side(z, exps):
        return power(z, exps)[:, :, None, :] * jnp.swapaxes(b_bar[z], 1, 2)[None]

    def col_side(z, exps):
        return power(z, exps)[:, :, None, :] * c_mat[z][None]

    gi = lax.broadcasted_iota(jnp.int32, (NP, L, 2, CH, P), 2)

    def pair_mat(x):
        x5 = jnp.transpose(x.reshape(L, NP, 2, CH, P), (1, 0, 2, 3, 4))
        blocks = [jnp.where(gi == h, x5, 0.0) for h in range(2)]
        return jnp.concatenate(blocks, axis=-1).reshape(NP, L * S5_PIECE, 2 * P)

    def cplx_pair(x, conj_sign):
        return jnp.concatenate([pair_mat(x.real), pair_mat(conj_sign * x.imag)], axis=-1)

    hi = lax.Precision.HIGHEST
    fwd = jnp.einsum('qap,qbp->qab', cplx_pair(row_side(0, mid - t), 1.0),
                     cplx_pair(col_side(0, t - mid), -1.0), precision=hi)
    bwd = jnp.einsum('qap,qbp->qab', cplx_pair(row_side(1, t - mid), 1.0),
                     cplx_pair(col_side(1, mid - t), -1.0), precision=hi)
    s_of = lax.broadcasted_iota(jnp.int32, (L * S5_PIECE, L * S5_PIECE), 0) // S5_PIECE
    t_of = lax.broadcasted_iota(jnp.int32, (L * S5_PIECE, L * S5_PIECE), 1) // S5_PIECE
    toep = jnp.where(t_of >= s_of, fwd, 0.0) + jnp.where(s_of >= t_of, bwd, 0.0)

    ball = jnp.concatenate([cplx_pair(row_side(0, L - 1.0 - t), 1.0), cplx_pair(row_side(1, t), 1.0)], axis=-1)
    call_t = jnp.concatenate([cplx_pair(col_side(0, t + 1.0), -1.0), cplx_pair(col_side(1, L - t), -1.0)], axis=-1)

    lam_l = jnp.exp(lam_dt * float(L))
    lam_l = jnp.stack([lam_l[0].real, lam_l[0].imag, lam_l[1].real, lam_l[1].imag])
    lam_l = jnp.transpose(lam_l.reshape(4, NP, 2 * P), (1, 0, 2)).reshape(NP, 1, 8 * P)
    return ball.astype(BF16), toep.astype(BF16), call_t.astype(BF16), lam_l


def _s5_scan(lhs, prm_all, layer, nb):
    npair, rows, w = lhs.shape
    ball, toep, call, lam_l = prm_all
    spec_w = pl.BlockSpec((1, 1, w, w), lambda q: (layer, q, 0, 0))
    return pl.pallas_call(
        functools.partial(_s5_kernel, nb=nb, ctx_chunks=CTX_LEN // S5_CHUNK),
        grid=(npair,),
        in_specs=[pl.BlockSpec((1, rows, w), lambda q: (q, 0, 0)), spec_w, spec_w, spec_w,
                  pl.BlockSpec((1, 1, 1, w), lambda q: (layer, q, 0, 0))],
        out_specs=pl.BlockSpec((1, rows, w), lambda q: (q, 0, 0)),
        out_shape=jax.ShapeDtypeStruct((npair, rows, w), F32),
        scratch_shapes=[pltpu.VMEM((4, rows, LANES), F32)],
        compiler_params=_cp(("parallel",)),
        name="s5_scan",
    )(lhs, ball, toep, call, lam_l)


def _glu_kernel(y_ref, u_ref, d_ref, w_ref, b_ref, o_ref, yn):
    nb = u_ref.shape[0]
    cpb = S5_PACK_ROWS // S5_CHUNK
    ppt = LANES // S5_PIECE
    ntile = BRANCH_W // LANES
    for t in range(S5_CHUNK):
        for j in range(ntile):
            tile = jnp.concatenate([y_ref[j * ppt + k, :, t * S5_PIECE:(t + 1) * S5_PIECE]
                                    for k in range(ppt)], axis=-1)
            yn[j, :, pl.ds(t, cpb, stride=S5_CHUNK), :] = tile.reshape(nb, cpb, LANES)
    yd = jnp.concatenate([yn[j] for j in range(ntile)], axis=-1)
    y = d_ref[...] * u_ref[...].astype(F32).reshape(nb * S5_PACK_ROWS, BRANCH_W)
    y = y + yd.reshape(nb * S5_PACK_ROWS, BRANCH_W)
    cdf = 0.5 * (1.0 + jnp.tanh(math.sqrt(2.0 / math.pi) * (y + 0.044715 * (y * y * y))))
    g = y * cdf
    z = _dot(g.astype(BF16), w_ref[0]) + b_ref[...]
    o_ref[...] = (g * _sigmoid(z)).astype(o_ref.dtype).reshape(o_ref.shape)


def _s5_glu(y, p1_3d, d_skip, w_glu_all, layer, b_glu):
    nb, tlen, _ = p1_3d.shape
    bw = BRANCH_W
    rows = nb * (S5_PACK_ROWS // S5_CHUNK)
    const = lambda i: (0, 0)
    blk = pl.BlockSpec((nb, S5_PACK_ROWS, bw), lambda i: (0, i, 0))
    return pl.pallas_call(
        _glu_kernel,
        grid=(tlen // S5_PACK_ROWS,),
        in_specs=[pl.BlockSpec((S5_NPAIR, rows, bw), lambda i: (0, i, 0)), blk,
                  pl.BlockSpec((1, bw), const), pl.BlockSpec((1, bw, bw), lambda i: (layer, 0, 0)),
                  pl.BlockSpec((1, bw), const)],
        out_specs=blk,
        out_shape=jax.ShapeDtypeStruct((nb, tlen, bw), BF16),
        scratch_shapes=[pltpu.VMEM((bw // LANES, nb, S5_PACK_ROWS, LANES), F32)],
        compiler_params=_cp(("parallel",)),
        name="s5_glu",
    )(y, p1_3d, d_skip.reshape(1, bw), w_glu_all, b_glu.reshape(1, bw))


def _lane_masks(rows):
    lane = lax.broadcasted_iota(jnp.int32, (rows, LANES), 1)
    return lane < HEAD_DIM, (lane % HEAD_DIM) < (HEAD_DIM // 2)


def _head_norm(x, gain, lo):
    ss = x * x
    s_lo = jnp.sum(jnp.where(lo, ss, 0.0), axis=-1, keepdims=True)
    s_hi = jnp.sum(jnp.where(lo, 0.0, ss), axis=-1, keepdims=True)
    ms = jnp.where(lo, s_lo, s_hi) * (1.0 / HEAD_DIM)
    return x * lax.rsqrt(ms + EPS) * gain


def _rope(x, cos, sin_signed, first_half):
    partner = jnp.where(first_half, pltpu.roll(x, LANES - HEAD_DIM // 2, 1),
                        pltpu.roll(x, HEAD_DIM // 2, 1))
    return x * cos + partner * sin_signed


def _prep_slab(x, gain, cos, sin_signed, scale):
    rows = x.shape[0]
    lo, first_half = _lane_masks(rows)
    y = _head_norm(x.astype(F32), gain, lo)
    if cos is not None:
        y = _rope(y, cos, sin_signed, first_half)
    if scale is not None:
        y = y * scale
    return y.astype(BF16)


def _prep_keys(k_ref, kn_ref, gain, cos_ref, sin_ref, nslab):
    tlen = k_ref.shape[1]
    for r0 in range(0, tlen, SEG_ROWS):
        rows = slice(r0, r0 + SEG_ROWS)
        for s in range(nslab):
            cols = slice(s * LANES, (s + 1) * LANES)
            cos = None if cos_ref is None else cos_ref[rows, :]
            sin = None if sin_ref is None else sin_ref[rows, :]
            kn_ref[rows, cols] = _prep_slab(k_ref[0, rows, cols], gain, cos, sin, None)


def _prep_values(v_ref, va_ref, nslab):
    tlen = v_ref.shape[1]
    for r0 in range(0, tlen, SEG_ROWS):
        rows = slice(r0, r0 + SEG_ROWS)
        lo, _ = _lane_masks(SEG_ROWS)
        for s in range(nslab):
            cols = slice(s * LANES, (s + 1) * LANES)
            v = v_ref[0, rows, cols]
            one = jnp.ones_like(v)
            va_ref[0, rows, cols] = jnp.where(lo, v, one)
            va_ref[1, rows, cols] = jnp.where(lo, one, v)


def _split_heads(q):
    lo, _ = _lane_masks(q.shape[0])
    zero = jnp.zeros_like(q)
    return jnp.where(lo, q, zero), jnp.where(lo, zero, q), lo


def _join_heads(o_lo, o_hi, extra_lo, extra_hi, lo):
    num = jnp.where(lo, o_lo, o_hi)
    den = pltpu.roll(jnp.where(lo, o_hi, o_lo), HEAD_DIM, 1)
    if extra_lo is not None:
        den = den + jnp.where(lo, extra_lo, extra_hi)
    return num / den


def _win_kernel(sink_ref, q_ref, k_ref, v_ref, cos_ref, sin_ref, qg_ref, kg_ref, o_ref,
                kn_ref, va_ref, qs_ref, e_ref):
    qi = pl.program_id(1)
    tq = q_ref.shape[1]
    tlen = k_ref.shape[1]
    span = tq + 2 * WINDOW
    nhead = WIN_HEADS

    @pl.when(qi == 0)
    def _():
        _prep_keys(k_ref, kn_ref, kg_ref[...], cos_ref, sin_ref, 1)
        _prep_values(v_ref, va_ref, 1)

    qrows = pl.ds(pl.multiple_of(qi * tq, tq), tq)
    cos_q = cos_ref[qrows, :]
    sin_q = sin_ref[qrows, :]

    def attend(local):
        for s in range(nhead // 2):
            cols = slice(s * LANES, (s + 1) * LANES)
            qn = _prep_slab(q_ref[0, :, cols], qg_ref[...], cos_q, sin_q, SCALE * LOG2E)
            q_lo, q_hi, _ = _split_heads(qn)
            qs_ref[(2 * s) * tq:(2 * s + 1) * tq, :] = q_lo
            qs_ref[(2 * s + 1) * tq:(2 * s + 2) * tq, :] = q_hi
        qs = qs_ref[...]
        s_ctx = _dot_t(qs, kn_ref[0:CTX_LEN, :])
        m = jnp.max(s_ctx, axis=-1, keepdims=True)
        if local:
            start = jnp.clip(qi * tq - WINDOW, CTX_LEN, tlen - span)
            krows = pl.ds(pl.multiple_of(start, LANES), span)
            qtok = qi * tq + lax.broadcasted_iota(jnp.int32, (tq, span), 0)
            ktok = start + lax.broadcasted_iota(jnp.int32, (tq, span), 1)
            valid = (jnp.abs(qtok - ktok) <= WINDOW)[None]
            s_loc = _dot_t(qs, kn_ref[krows, :]).reshape(nhead, tq, span)
            s_loc = jnp.where(valid, s_loc, NEG_INF).reshape(nhead * tq, span)
            m = jnp.maximum(m, jnp.max(s_loc, axis=-1, keepdims=True))
        sinks = [sink_ref[(hb // 2) + (hb % 2) * (nhead // 2)] * LOG2E for hb in range(nhead)]
        m = jnp.concatenate([jnp.maximum(m[hb * tq:(hb + 1) * tq], sinks[hb]) for hb in range(nhead)], axis=0)
        e_ref[:, 0:CTX_LEN] = jnp.exp2(s_ctx - m).astype(BF16)
        if local:
            e_ref[:, CTX_LEN:CTX_LEN + span] = jnp.exp2(s_loc - m).astype(BF16)
        lo, _ = _lane_masks(tq)
        for s in range(nhead // 2):
            outs, extra = [], []
            for half in range(2):
                hb = 2 * s + half
                rows = slice(hb * tq, (hb + 1) * tq)
                o = _dot(e_ref[rows, 0:CTX_LEN], va_ref[half, 0:CTX_LEN, :])
                if local:
                    o = o + _dot(e_ref[rows, CTX_LEN:CTX_LEN + span], va_ref[half, krows, :])
                outs.append(o)
                extra.append(jnp.exp2(sinks[hb] - m[rows]))
            o_ref[0, :, s * LANES:(s + 1) * LANES] = _join_heads(
                outs[0], outs[1], extra[0], extra[1], lo).astype(o_ref.dtype)

    @pl.when(qi < CTX_LEN // tq)
    def _():
        attend(False)

    @pl.when(qi >= CTX_LEN // tq)
    def _():
        attend(True)


def _win_attn(p1, sink, cos_t, sin_t, qg, kg, tq=128):
    bsz, tlen, _ = p1.shape
    kcol = 8 * BRANCH_W // LANES
    return pl.pallas_call(
        _win_kernel,
        grid=(bsz, tlen // tq),
        in_specs=[pl.BlockSpec(memory_space=pltpu.SMEM),
                  pl.BlockSpec((1, tq, BRANCH_W), lambda b, i: (b, i, 1)),
                  pl.BlockSpec((1, tlen, LANES), lambda b, i: (b, 0, kcol)),
                  pl.BlockSpec((1, tlen, LANES), lambda b, i: (b, 0, kcol + 1)),
                  pl.BlockSpec((tlen, LANES), lambda b, i: (0, 0)),
                  pl.BlockSpec((tlen, LANES), lambda b, i: (0, 0)),
                  pl.BlockSpec((1, LANES), lambda b, i: (0, 0)),
                  pl.BlockSpec((1, LANES), lambda b, i: (0, 0))],
        out_specs=pl.BlockSpec((1, tq, BRANCH_W), lambda b, i: (b, i, 0)),
        out_shape=jax.ShapeDtypeStruct((bsz, tlen, BRANCH_W), BF16),
        scratch_shapes=[pltpu.VMEM((tlen, LANES), BF16), pltpu.VMEM((2, tlen, LANES), BF16),
                        pltpu.VMEM((WIN_HEADS * tq, LANES), BF16),
                        pltpu.VMEM((WIN_HEADS * tq, CTX_LEN + tq + 2 * WINDOW), BF16)],
        compiler_params=_cp(("parallel", "arbitrary")),
        name="win_attn",
    )(sink, p1, p1, p1, cos_t, sin_t, qg, kg)


def _diff_kernel(lam_ref, q_ref, k_ref, v_ref, cos_ref, sin_ref, qg_ref, kg_ref, sg_ref, o_ref, kn_ref):
    qi = pl.program_id(1)
    tq = q_ref.shape[1]
    tlen = k_ref.shape[1]
    lam = lam_ref[0]
    out_scale = lam_ref[1]

    @pl.when(qi == 0)
    def _():
        _prep_keys(k_ref, kn_ref, kg_ref[...], cos_ref, sin_ref, DIFF_HEADS)

    qrows = pl.ds(pl.multiple_of(qi * tq, tq), tq)
    cos_q = cos_ref[qrows, :]
    sin_q = sin_ref[qrows, :]

    def attend(nk):
        for h in range(DIFF_HEADS):
            cols = slice(h * LANES, (h + 1) * LANES)
            qn = _prep_slab(q_ref[0, :, cols], qg_ref[...], cos_q, sin_q, SCALE * LOG2E)
            q1, q2, _ = _split_heads(qn)
            keys = kn_ref[0:nk, cols]
            vals = v_ref[0, 0:nk, cols]
            s1 = _dot_t(q1, keys)
            s2 = _dot_t(q2, keys)
            e1 = jnp.exp2(s1 - jnp.max(s1, axis=-1, keepdims=True))
            e2 = jnp.exp2(s2 - jnp.max(s2, axis=-1, keepdims=True))
            r1 = 1.0 / _rowsum(e1)
            r2 = lam / _rowsum(e2)
            o = _dot(e1.astype(BF16), vals) * r1 - _dot(e2.astype(BF16), vals) * r2
            o = o * lax.rsqrt(jnp.mean(o * o, axis=-1, keepdims=True) + EPS) * sg_ref[...]
            o_ref[0, :, cols] = (o * out_scale).astype(o_ref.dtype)

    @pl.when(qi < CTX_LEN // tq)
    def _():
        attend(CTX_LEN)

    @pl.when(qi >= CTX_LEN // tq)
    def _():
        attend(tlen)


def _diff_attn(p1, lam_vec, cos_t, sin_t, qg, kg, sg, tq=256):
    bsz, tlen, _ = p1.shape
    full = lambda c: pl.BlockSpec((1, tlen, BRANCH_W), functools.partial(lambda b, i, c: (b, 0, c), c=c))
    const = lambda b, i: (0, 0)
    return pl.pallas_call(
        _diff_kernel,
        grid=(bsz, tlen // tq),
        in_specs=[pl.BlockSpec(memory_space=pltpu.SMEM),
                  pl.BlockSpec((1, tq, BRANCH_W), lambda b, i: (b, i, 2)),
                  full(3), full(4),
                  pl.BlockSpec((tlen, LANES), const), pl.BlockSpec((tlen, LANES), const),
                  pl.BlockSpec((1, LANES), const), pl.BlockSpec((1, LANES), const),
                  pl.BlockSpec((1, LANES), const)],
        out_specs=pl.BlockSpec((1, tq, BRANCH_W), lambda b, i: (b, i, 0)),
        out_shape=jax.ShapeDtypeStruct((bsz, tlen, BRANCH_W), BF16),
        scratch_shapes=[pltpu.VMEM((tlen, BRANCH_W), BF16)],
        compiler_params=_cp(("parallel", "arbitrary")),
        name="diff_attn",
    )(lam_vec, p1, p1, p1, cos_t, sin_t, qg, kg, sg)


def _na_kernel(q_ref, k_ref, v_ref, bias_ref, qg_ref, kg_ref, o_ref, kn_ref, va_ref, qs_ref, sc_ref, e_ref,
               *, grid_rows):
    qi = pl.program_id(1)
    span = NA_WIN_R * GRID_W
    nslab = NA_HEADS // 2

    @pl.when(qi == 0)
    def _():
        _prep_keys(k_ref, kn_ref, kg_ref[...], None, None, nslab)
        _prep_values(v_ref, va_ref, nslab)

    def attend(qrows, nq, krows):
        for s in range(nslab):
            cols = slice(s * LANES, (s + 1) * LANES)
            qn = _prep_slab(q_ref[0, qrows, cols], qg_ref[...], None, None, SCALE * LOG2E)
            q_lo, q_hi, _ = _split_heads(qn)
            qs_ref[(2 * s) * nq:(2 * s + 1) * nq, :] = q_lo
            qs_ref[(2 * s + 1) * nq:(2 * s + 2) * nq, :] = q_hi
        ncol = CTX_LEN if krows is None else CTX_LEN + span
        for s in range(nslab):
            cols = slice(s * LANES, (s + 1) * LANES)
            rows = slice(2 * s * nq, (2 * s + 2) * nq)
            sc_ref[rows, 0:CTX_LEN] = _dot_t(qs_ref[rows, :], kn_ref[0:CTX_LEN, cols])
            if krows is not None:
                bias = bias_ref[0, 2 * s:2 * s + 2].reshape(2 * nq, span)
                sc_ref[rows, CTX_LEN:ncol] = _dot_t(qs_ref[rows, :], kn_ref[krows, cols]) + bias
        nrow = NA_HEADS * nq
        sc = sc_ref[0:nrow, 0:ncol]
        e_ref[0:nrow, 0:ncol] = jnp.exp2(sc - jnp.max(sc, axis=-1, keepdims=True)).astype(BF16)
        lo, _ = _lane_masks(nq)
        for s in range(nslab):
            cols = slice(s * LANES, (s + 1) * LANES)
            outs = []
            for half in range(2):
                rows = slice((2 * s + half) * nq, (2 * s + half + 1) * nq)
                o = _dot(e_ref[rows, 0:CTX_LEN], va_ref[half, 0:CTX_LEN, cols])
                if krows is not None:
                    o = o + _dot(e_ref[rows, CTX_LEN:ncol], va_ref[half, krows, cols])
                outs.append(o)
            o_ref[0, qrows, cols] = _join_heads(outs[0], outs[1], None, None, lo).astype(o_ref.dtype)

    @pl.when(qi == 0)
    def _():
        attend(slice(0, CTX_LEN), CTX_LEN, None)

    @pl.when(qi > 0)
    def _():
        r = qi - 1
        qrows = pl.ds(pl.multiple_of(CTX_LEN + r * GRID_W, GRID_W), GRID_W)
        k0 = jnp.clip(r - NA_WIN_R // 2, 0, grid_rows - NA_WIN_R)
        krows = pl.ds(pl.multiple_of(CTX_LEN + k0 * GRID_W, GRID_W), span)
        attend(qrows, GRID_W, krows)


def _na_bias(rpb_all, rows):
    win_r = NA_WIN_R
    half = win_r // 2
    r = jnp.concatenate([jnp.arange(half + 1), jnp.arange(rows - half + 1, rows)])
    row_idx = jnp.clip(r - half, 0, rows - win_r)[:, None] + jnp.arange(win_r)[None, :]
    r_off = row_idx - r[:, None] + (NA_WIN_R - 1)
    col = jnp.arange(GRID_W)
    c_off = jnp.clip(col[None, :] - col[:, None] + (NA_WIN_C - 1), 0, 2 * NA_WIN_C - 2)
    onehot = (c_off[None] == jnp.arange(2 * NA_WIN_C - 1)[:, None, None]).astype(F32)
    rows_sel = rpb_all.astype(F32)[:, :, r_off] * LOG2E
    bias = jnp.einsum('lhpwk,kqc->lphqwc', rows_sel, onehot, precision=lax.Precision.HIGHEST)
    col_start = jnp.clip(col - NA_WIN_C // 2, 0, GRID_W - NA_WIN_C)
    col_ok = (col[None, :] >= col_start[:, None]) & (col[None, :] < col_start[:, None] + NA_WIN_C)
    bias = jnp.where(col_ok[None, None, None, :, None, :], bias, NEG_INF)
    depth = rpb_all.shape[0]
    return bias.reshape(depth, r.shape[0], NA_HEADS, GRID_W, win_r * GRID_W)


def _na_attn(p1, bias_all, layer, qg, kg):
    bsz, tlen, _ = p1.shape
    rows = (tlen - CTX_LEN) // GRID_W
    half = NA_WIN_R // 2
    full = lambda c: pl.BlockSpec((1, tlen, BRANCH_W), functools.partial(lambda b, i, c: (b, 0, c), c=c))
    const = lambda b, i: (0, 0)

    def bias_idx(b, i):
        r = jnp.maximum(i - 1, 0)
        return (layer, jnp.minimum(r, half) + jnp.maximum(r - (rows - half), 0), 0, 0, 0)

    return pl.pallas_call(
        functools.partial(_na_kernel_wrap, grid_rows=rows),
        grid=(bsz, rows + 1),
        in_specs=[full(5), full(6), full(7),
                  pl.BlockSpec((1, 1, NA_HEADS, GRID_W, NA_WIN_R * GRID_W), bias_idx),
                  pl.BlockSpec((1, LANES), const), pl.BlockSpec((1, LANES), const)],
        out_specs=pl.BlockSpec((1, tlen, BRANCH_W), lambda b, i: (b, 0, 0)),
        out_shape=jax.ShapeDtypeStruct((bsz, tlen, BRANCH_W), BF16),
        scratch_shapes=[pltpu.VMEM((tlen, BRANCH_W), BF16), pltpu.VMEM((2, tlen, BRANCH_W), BF16),
                        pltpu.VMEM((NA_HEADS * CTX_LEN, LANES), BF16),
                        pltpu.VMEM((NA_HEADS * CTX_LEN, CTX_LEN + NA_WIN_R * GRID_W), F32),
                        pltpu.VMEM((NA_HEADS * CTX_LEN, CTX_LEN + NA_WIN_R * GRID_W), BF16)],
        compiler_params=_cp(("parallel", "arbitrary")),
        name="na_attn",
    )(p1, p1, p1, bias_all, qg, kg)


def _na_kernel_wrap(q_ref, k_ref, v_ref, bias_ref, *rest, grid_rows):
    _na_kernel(q_ref, k_ref, v_ref, bias_ref.at[0], *rest, grid_rows=grid_rows)


def _row_copy(src_hbm, row, dst, r, sem):
    return pltpu.make_async_copy(src_hbm.at[pl.ds(row, 1), :], dst.at[pl.ds(r, 1), :], sem)


def _gather_rows(idx_ref, base, src_hbm, dst, sem, nrows):
    for r in range(nrows):
        _row_copy(src_hbm, idx_ref[base + r], dst, r, sem).start()


def _gather_wait(src_hbm, dst, sem, nrows):
    pltpu.make_async_copy(src_hbm.at[pl.ds(0, nrows), :], dst, sem).wait()


def _expert_kernel(te_ref, rt_ref, na_ref, h_hbm, rw_ref, w1_ref, w3_ref, w2_ref, o_ref, buf_a, buf_b, sem):
    i = pl.program_id(0)
    tm = buf_a.shape[0]
    even = i % 2 == 0
    nact = jnp.maximum(na_ref[0], 1)

    @pl.when(i == 0)
    def _():
        _gather_rows(rt_ref, 0, h_hbm, buf_a, sem.at[0], tm)

    def step(cur, cur_sem, nxt, nxt_sem):
        _gather_wait(h_hbm, cur, cur_sem, tm)
        nxt_tile = jnp.minimum(i + 1, pl.num_programs(0) - 1)
        _gather_rows(rt_ref, nxt_tile * tm, h_hbm, nxt, nxt_sem, tm)
        x = cur[...].astype(BF16)
        a = _dot(x, w1_ref[0, 0])
        mid = (a * jax.nn.sigmoid(a)) * _dot(x, w3_ref[0, 0])
        o_ref[...] = _dot(mid.astype(BF16), w2_ref[0, 0]) * rw_ref[...]

    @pl.when((i < nact) & even)
    def _():
        step(buf_a, sem.at[0], buf_b, sem.at[1])

    @pl.when((i < nact) & jnp.logical_not(even))
    def _():
        step(buf_b, sem.at[1], buf_a, sem.at[0])

    @pl.when((i == nact - 1) & even)
    def _():
        _gather_wait(h_hbm, buf_b, sem.at[1], tm)

    @pl.when((i == nact - 1) & jnp.logical_not(even))
    def _():
        _gather_wait(h_hbm, buf_a, sem.at[0], tm)

    @pl.when(i >= nact)
    def _():
        o_ref[...] = jnp.zeros_like(o_ref)


def _moe_experts(h, tile_expert, row_token, n_active, row_weight, w1_all, w3_all, w2_all, layer, tm=MOE_TM):
    n, d = h.shape
    rmax = row_token.shape[0]
    ff = w1_all.shape[3]
    grid_spec = pltpu.PrefetchScalarGridSpec(
        num_scalar_prefetch=3,
        grid=(rmax // tm,),
        in_specs=[pl.BlockSpec(memory_space=pl.ANY),
                  pl.BlockSpec((tm, 1), lambda i, te, rt, na: (i, 0)),
                  pl.BlockSpec((1, 1, d, ff), lambda i, te, rt, na: (layer, te[i], 0, 0)),
                  pl.BlockSpec((1, 1, d, ff), lambda i, te, rt, na: (layer, te[i], 0, 0)),
                  pl.BlockSpec((1, 1, ff, d), lambda i, te, rt, na: (layer, te[i], 0, 0))],
        out_specs=pl.BlockSpec((tm, d), lambda i, te, rt, na: (i, 0)),
        scratch_shapes=[pltpu.VMEM((tm, d), F32), pltpu.VMEM((tm, d), F32), pltpu.SemaphoreType.DMA((2,))])
    return pl.pallas_call(
        _expert_kernel,
        grid_spec=grid_spec,
        out_shape=jax.ShapeDtypeStruct((rmax, d), F32),
        compiler_params=_cp(("arbitrary",)),
        name="moe_experts",
    )(tile_expert, row_token, n_active, h, row_weight, w1_all, w3_all, w2_all)


def _combine_kernel(d0_ref, d1_ref, ys_hbm, x_ref, mod_ref, *rest, blocks_per_batch, next_norm):
    if next_norm:
        g_ref, sh_ref, sc_ref, o_ref, h_ref, buf0, buf1, sem = rest
    else:
        o_ref, buf0, buf1, sem = rest
    i = pl.program_id(0)
    n = pl.num_programs(0)
    tm = x_ref.shape[0]
    slot = i % 2

    def issue(tile, s):
        _gather_rows(d0_ref, tile * tm, ys_hbm, buf0.at[s], sem.at[0, s], tm)
        _gather_rows(d1_ref, tile * tm, ys_hbm, buf1.at[s], sem.at[1, s], tm)

    @pl.when(i == 0)
    def _():
        issue(0, 0)

    @pl.when(i + 1 < n)
    def _():
        issue(i + 1, 1 - slot)

    _gather_wait(ys_hbm, buf0.at[slot], sem.at[0, slot], tm)
    _gather_wait(ys_hbm, buf1.at[slot], sem.at[1, slot], tm)
    row = _seg_mod_row(i, blocks_per_batch)
    x_new = x_ref[...] + mod_ref[row] * (buf0[slot] + buf1[slot])
    o_ref[...] = x_new
    if next_norm:
        h_ref[...] = _mod_norm(x_new, g_ref[...], sh_ref[row], sc_ref[row]).astype(h_ref.dtype)


def _moe_combine(ys, dest0, dest1, x, modtab, k_mod, blocks_per_batch, next_norm=None):
    n, d = x.shape
    tm = SEG_ROWS
    nrow = modtab.shape[0]
    row_blk = pl.BlockSpec((tm, d), lambda i, a, b: (i, 0))
    mod_spec = lambda k: pl.BlockSpec((nrow, 1, d), lambda i, a, b: (0, 0, k))
    in_specs = [pl.BlockSpec(memory_space=pl.ANY), row_blk, mod_spec(k_mod)]
    args = [dest0, dest1, ys, x, modtab]
    out_specs = row_blk
    out_shape = jax.ShapeDtypeStruct((n, d), F32)
    if next_norm is not None:
        gain, modtab_next, k_shift, k_scale = next_norm
        in_specs += [pl.BlockSpec((1, d), lambda i, a, b: (0, 0)), mod_spec(k_shift), mod_spec(k_scale)]
        args += [gain.reshape(1, d), modtab_next, modtab_next]
        out_specs = [row_blk, row_blk]
        out_shape = [out_shape, jax.ShapeDtypeStruct((n, d), BF16)]
    grid_spec = pltpu.PrefetchScalarGridSpec(
        num_scalar_prefetch=2,
        grid=(n // tm,),
        in_specs=in_specs,
        out_specs=out_specs,
        scratch_shapes=[pltpu.VMEM((2, tm, d), F32), pltpu.VMEM((2, tm, d), F32),
                        pltpu.SemaphoreType.DMA((2, 2))])
    return pl.pallas_call(
        functools.partial(_combine_kernel, blocks_per_batch=blocks_per_batch, next_norm=next_norm is not None),
        grid_spec=grid_spec,
        out_shape=out_shape,
        compiler_params=_cp(("arbitrary",)),
        name="moe_combine",
    )(*args)


def _route(logits, tm=MOE_TM):
    n = logits.shape[0]
    g_prob = jax.nn.softmax(logits[:, :N_GROUPS], axis=-1)
    g_idx = jnp.argmax(g_prob, axis=-1).astype(jnp.int32)
    g_w = jnp.max(g_prob, axis=-1)
    e_logits = logits[:, N_GROUPS:N_GROUPS + N_EXPERTS].reshape(n, N_GROUPS, EXPERTS_PER_GROUP)
    in_group = jnp.arange(N_GROUPS, dtype=jnp.int32)[None, :, None] == g_idx[:, None, None]
    e_in = jnp.sum(jnp.where(in_group, e_logits, 0.0), axis=1)
    slot = jnp.arange(EXPERTS_PER_GROUP, dtype=jnp.int32)[None, :]
    i1 = jnp.argmax(e_in, axis=-1).astype(jnp.int32)
    v1 = jnp.max(e_in, axis=-1)
    rest = jnp.where(slot == i1[:, None], -jnp.inf, e_in)
    i2 = jnp.argmax(rest, axis=-1).astype(jnp.int32)
    v2 = jnp.max(rest, axis=-1)
    w_sel = jax.nn.softmax(jnp.stack([v1, v2], axis=-1), axis=-1) * g_w[:, None]
    expert = g_idx[:, None] * EXPERTS_PER_GROUP + jnp.stack([i1, i2], axis=-1)

    e_flat = expert.reshape(-1)
    w_flat = w_sel.reshape(-1)
    ids = jnp.arange(N_EXPERTS, dtype=jnp.int32)
    onehot = (e_flat[:, None] == ids[None, :]).astype(jnp.int32)
    csum = jnp.cumsum(onehot, axis=0)
    rank = jnp.sum(onehot * csum, axis=1) - 1
    counts = csum[-1]
    padded = ((counts + tm - 1) // tm) * tm
    pend = jnp.cumsum(padded)
    pstart = pend - padded
    ustart = jnp.cumsum(counts) - counts
    dest = (jnp.sum(onehot * pstart[None, :], axis=1) + rank).astype(jnp.int32)

    rmax = 2 * n + N_EXPERTS * tm
    total = pend[-1]
    tile_start = jnp.arange(rmax // tm, dtype=jnp.int32) * tm
    tile_expert = jnp.minimum(jnp.sum((tile_start[:, None] >= pend[None, :]).astype(jnp.int32), axis=1),
                              N_EXPERTS - 1)
    order = jnp.argsort(e_flat, stable=True).astype(jnp.int32)
    r = jnp.arange(rmax, dtype=jnp.int32)
    e_r = jnp.minimum(jnp.sum((r[:, None] >= pend[None, :]).astype(jnp.int32), axis=1), N_EXPERTS - 1)
    oh_r = (e_r[:, None] == ids[None, :]).astype(jnp.int32)
    off = r - jnp.sum(oh_r * pstart[None, :], axis=1)
    valid = (r < total) & (off < jnp.sum(oh_r * counts[None, :], axis=1))
    src = order[jnp.clip(jnp.sum(oh_r * ustart[None, :], axis=1) + off, 0, 2 * n - 1)]
    row_token = jnp.where(valid, src // 2, 0).astype(jnp.int32)
    row_weight = jnp.where(valid, w_flat[src], 0.0)
    n_active = (total // tm).astype(jnp.int32)
    last_expert = tile_expert[jnp.maximum(n_active - 1, 0)]
    tile_expert = jnp.where(tile_start < total, tile_expert, last_expert).astype(jnp.int32)
    dest2 = dest.reshape(n, 2)
    return (tile_expert, row_token, n_active.reshape(1), row_weight.reshape(rmax, 1),
            dest2[:, 0], dest2[:, 1])


def _rope_tables(n_lat):
    t = jnp.arange(n_lat, dtype=jnp.int32)
    row = (t // GRID_W).astype(F32)
    col = (t % GRID_W).astype(F32)
    per_axis = HEAD_DIM // 4
    inv_freq = ROPE_BASE ** (-jnp.arange(per_axis, dtype=F32) / per_axis)
    ang = jnp.concatenate([row[:, None] * inv_freq, col[:, None] * inv_freq], axis=-1)
    cos = jnp.concatenate([jnp.ones((CTX_LEN, HEAD_DIM // 2), F32), jnp.cos(ang)], axis=0)
    sin = jnp.concatenate([jnp.zeros((CTX_LEN, HEAD_DIM // 2), F32), jnp.sin(ang)], axis=0)
    cos_t = jnp.tile(cos, (1, 4))
    sin_t = jnp.tile(jnp.concatenate([-sin, sin], axis=-1), (1, 2))
    return cos_t, sin_t


def _win_head_perm():
    cols = []
    for s in range(WIN_HEADS // 2):
        for half in range(2):
            h = s + half * (WIN_HEADS // 2)
            cols.extend(range(h * HEAD_DIM, (h + 1) * HEAD_DIM))
    return cols


def _p1_weight(w_in):
    bw = BRANCH_W
    kvw = WIN_KV_HEADS * HEAD_DIM
    hd = HEAD_DIM
    cols = lambda a, b: w_in[:, :, a:b].astype(BF16)
    q_b = [cols(bw + h * hd, bw + (h + 1) * hd) for s in range(WIN_HEADS // 2) for h in (s, s + WIN_HEADS // 2)]
    kv0 = 2 * bw
    rest0 = kv0 + 2 * kvw
    pad = jnp.zeros(w_in.shape[:2] + (bw - 2 * kvw,), BF16)
    p1 = jnp.concatenate([cols(0, bw)] + q_b + [cols(rest0, rest0 + 6 * bw), cols(kv0, rest0), pad], axis=2)
    return p1, cols(rest0 + 6 * bw, w_in.shape[2])


def kernel(x, c, ctx, c_ctx, w_ada, b_ada, norm_mix, norm_ffn, w_in, s5_a_re, s5_a_im, s5_log_step, s5_b_re, s5_b_im, s5_c_re, s5_c_im, s5_d, s5_w_glu, s5_b_glu, win_qn, win_kn, win_sink, diff_qn, diff_kn, diff_lambda, diff_subln, na_qn, na_kn, na_rpb, w_branch, w_out, moe_w_group, moe_b_group, moe_w_expert, moe_b_expert, moe_w1, moe_w3, moe_w2):
    bsz, n_lat, d = x.shape
    tlen = CTX_LEN + n_lat
    ntok = bsz * tlen
    bpb = tlen // SEG_ROWS
    depth = w_ada.shape[0]

    cond = jnp.zeros((ADA_ROWS, d), F32).at[:bsz].set(c).at[bsz].set(c_ctx)
    mod = _ada_mod(cond, w_ada, b_ada)
    mod_ctx = jnp.broadcast_to(mod[:, bsz:bsz + 1], (depth, bsz, 6 * d))
    modtab = jnp.stack([mod_ctx, mod[:, :bsz]], axis=2).reshape(depth, 2 * bsz, 1, 6 * d)

    w_p1, w_gate = _p1_weight(w_in)
    wb_all = w_branch.astype(BF16)
    wb_win = jnp.concatenate([wb_all[:, 1:2, h * HEAD_DIM:(h + 1) * HEAD_DIM]
                              for s in range(WIN_HEADS // 2) for h in (s, s + WIN_HEADS // 2)], axis=2)
    wb_all = jnp.concatenate([wb_all[:, :1], wb_win, wb_all[:, 2:]], axis=1)
    w_out_bf = w_out.astype(BF16)
    w_glu_bf = s5_w_glu.astype(BF16)
    w1_bf, w3_bf, w2_bf = moe_w1.astype(BF16), moe_w3.astype(BF16), moe_w2.astype(BF16)
    w_r = jnp.zeros((depth, d, LANES), F32).at[:, :, :N_GROUPS].set(moe_w_group)
    w_r = w_r.at[:, :, N_GROUPS:N_GROUPS + N_EXPERTS].set(moe_w_expert)
    rb = jnp.zeros((depth, 1, LANES), F32).at[:, 0, :N_GROUPS].set(moe_b_group)
    rb = rb.at[:, 0, N_GROUPS:N_GROUPS + N_EXPERTS].set(moe_b_expert)
    wr_hi, wr_lo = _split_bf16(w_r)
    na_bias = _na_bias(na_rpb, n_lat // GRID_W)
    s5_prm = jax.vmap(_s5_params)(s5_a_re, s5_a_im, s5_log_step, s5_b_re, s5_b_im, s5_c_re, s5_c_im)

    cos_t, sin_t = _rope_tables(n_lat)
    tile2 = lambda g: jnp.tile(g.astype(F32), 2).reshape(1, LANES)
    xs = jnp.concatenate([ctx, x], axis=1)

    for l in range(depth):
        lam_init = 0.8 - 0.6 * math.exp(-0.3 * l)
        mt = modtab[l]

        if l == 0:
            h = _norm(xs, norm_mix[l], mt, 0, 1).reshape(ntok, d)
        p1 = _matmul(h, w_p1, l, BF16, tn=P1_WIDTH // 3)
        gates = _matmul(h, w_gate, l, BF16, tn=1024)
        p1_3d = p1.reshape(bsz, tlen, P1_WIDTH)

        y_s5 = _s5_scan(_s5_pack(p1_3d), s5_prm, l, bsz)
        y_a = _s5_glu(y_s5, p1_3d, s5_d[l], w_glu_bf, l, s5_b_glu[l]).reshape(ntok, BRANCH_W)

        y_b = _win_attn(p1_3d, win_sink[l].astype(F32), cos_t, sin_t,
                        tile2(win_qn[l]), tile2(win_kn[l])).reshape(ntok, BRANCH_W)

        lp = diff_lambda[l].astype(F32)
        lam = jnp.exp(jnp.sum(lp[0] * lp[1])) - jnp.exp(jnp.sum(lp[2] * lp[3])) + lam_init
        lam_vec = jnp.stack([lam, jnp.asarray(1.0 - lam_init, F32)])
        y_c = _diff_attn(p1_3d, lam_vec, cos_t, sin_t, tile2(diff_qn[l]), tile2(diff_kn[l]),
                         diff_subln[l].astype(F32).reshape(1, LANES)).reshape(ntok, BRANCH_W)

        y_d = _na_attn(p1_3d, na_bias, l, tile2(na_qn[l]), tile2(na_kn[l])).reshape(ntok, BRANCH_W)

        merged = _merge((y_a, y_b, y_c, y_d), gates, wb_all, l)
        xs = _matmul_resid(merged, w_out_bf, l, xs.reshape(ntok, d), mt, 2, bpb, tn=1024)
        xs = xs.reshape(bsz, tlen, d)

        hf, logits = _norm_router(xs, norm_ffn[l], mt, 3, 4, wr_hi[l], wr_lo[l], rb[l])
        te, rt, na, rw, d0, d1 = _route(logits.reshape(ntok, LANES))
        ys = _moe_experts(hf.reshape(ntok, d), te, rt, na, rw, w1_bf, w3_bf, w2_bf, l)
        if l + 1 < depth:
            xs, h = _moe_combine(ys, d0, d1, xs.reshape(ntok, d), mt, 5, bpb,
                                 next_norm=(norm_mix[l + 1], modtab[l + 1], 0, 1))
        else:
            xs = _moe_combine(ys, d0, d1, xs.reshape(ntok, d), mt, 5, bpb)
        xs = xs.reshape(bsz, tlen, d)

    return xs[:, CTX_LEN:]
```

```python
import functools
import math

import jax
import jax.numpy as jnp
from jax import lax
from jax.experimental import pallas as pl
from jax.experimental.pallas import tpu as pltpu

F32 = jnp.float32
BF16 = jnp.bfloat16

D_MODEL = 2048
DEPTH = 4
GRID_W = 64
CTX_LEN = 256
HEAD_DIM = 64
BRANCH_W = 512
N_BRANCH = 4
S5_CH = 16
S5_GROUPS = BRANCH_W // S5_CH
S5_STATE = 64
WIN_HEADS = 8
WIN_KV_HEADS = 2
WINDOW = 128
DIFF_HEADS = 4
NA_HEADS = 8
NA_WIN_R = 8
NA_WIN_C = 16
N_GROUPS = 4
EXPERTS_PER_GROUP = 4
N_EXPERTS = 16
EXPERT_FF = D_MODEL // 2
ROPE_BASE = 100.0
EPS = 1e-6
NEG_INF = -1e30

LANES = 128
SEG_ROWS = 256
ADA_ROWS = 16
P1_WIDTH = 9 * BRANCH_W
MOE_TM = 256
VMEM_LIMIT = 52 * 1024 * 1024

S5_CHUNK = 16
S5_PACK_ROWS = 128
S5_NPAIR = S5_GROUPS // 2
S5_PIECE = 2 * S5_CH

NA_ROWS_PER_STEP = 2

LOG2E = 1.4426950408889634
SCALE = HEAD_DIM ** -0.5


def _cp(sem, vmem=VMEM_LIMIT):
    return pltpu.CompilerParams(dimension_semantics=sem, vmem_limit_bytes=vmem)


def _dot(a, b):
    return jnp.dot(a, b, preferred_element_type=F32)


def _dot_t(a, b):
    return lax.dot_general(a, b, (((1,), (1,)), ((), ())), preferred_element_type=F32)


def _split_bf16(x):
    hi = x.astype(BF16)
    lo = (x - hi.astype(F32)).astype(BF16)
    return hi, lo


def _sigmoid(x):
    return 0.5 * jnp.tanh(0.5 * x) + 0.5


def _rowsum(x):
    acc = x[:, :LANES]
    for j in range(1, x.shape[1] // LANES):
        acc = acc + x[:, j * LANES:(j + 1) * LANES]
    return jnp.sum(acc, axis=-1, keepdims=True)


def _ada_kernel(c_ref, w_ref, b_ref, o_ref):
    c = c_ref[...]
    a_hi, a_lo = _split_bf16(c * jax.nn.sigmoid(c))
    w_hi, w_lo = _split_bf16(w_ref[0])
    o_ref[0] = _dot(a_hi, w_hi) + _dot(a_lo, w_hi) + _dot(a_hi, w_lo) + b_ref[0]


def _ada_mod(cond, w_ada, b_ada):
    depth, d, n = w_ada.shape
    tn = 1024
    return pl.pallas_call(
        _ada_kernel,
        grid=(depth, n // tn),
        in_specs=[pl.BlockSpec((ADA_ROWS, d), lambda l, j: (0, 0)),
                  pl.BlockSpec((1, d, tn), lambda l, j: (l, 0, j)),
                  pl.BlockSpec((1, 1, tn), lambda l, j: (l, 0, j))],
        out_specs=pl.BlockSpec((1, ADA_ROWS, tn), lambda l, j: (l, 0, j)),
        out_shape=jax.ShapeDtypeStruct((depth, ADA_ROWS, n), F32),
        compiler_params=_cp(("arbitrary", "arbitrary")),
        name="ada_mod",
    )(cond, w_ada, b_ada.reshape(depth, 1, n))


def _mod_norm(x, gain, sh, sc):
    y = x * lax.rsqrt(jnp.mean(x * x, axis=-1, keepdims=True) + EPS) * gain
    return y * (1.0 + sc) + sh


def _norm_kernel(x_ref, g_ref, sh_ref, sc_ref, o_ref):
    o_ref[0] = _mod_norm(x_ref[0], g_ref[...], sh_ref[0], sc_ref[0]).astype(o_ref.dtype)


def _norm_router_kernel(x_ref, g_ref, sh_ref, sc_ref, whi_ref, wlo_ref, rb_ref, o_ref, lg_ref):
    h = _mod_norm(x_ref[0], g_ref[...], sh_ref[0], sc_ref[0])
    o_ref[0] = h
    h_hi, h_lo = _split_bf16(h)
    lg_ref[0] = (_dot(h_hi, whi_ref[...]) + _dot(h_lo, whi_ref[...]) + _dot(h_hi, wlo_ref[...])
                 + rb_ref[...])


def _norm_specs(d, k_shift, k_scale):
    def mod_spec(k):
        return pl.BlockSpec((1, 1, d), lambda b, t: (2 * b + jnp.minimum(t, 1), 0, k))
    return [pl.BlockSpec((1, SEG_ROWS, d), lambda b, t: (b, t, 0)),
            pl.BlockSpec((1, d), lambda b, t: (0, 0)),
            mod_spec(k_shift), mod_spec(k_scale)]


def _norm(xs, gain, modtab, k_shift, k_scale):
    bsz, tlen, d = xs.shape
    return pl.pallas_call(
        _norm_kernel,
        grid=(bsz, tlen // SEG_ROWS),
        in_specs=_norm_specs(d, k_shift, k_scale),
        out_specs=pl.BlockSpec((1, SEG_ROWS, d), lambda b, t: (b, t, 0)),
        out_shape=jax.ShapeDtypeStruct((bsz, tlen, d), BF16),
        compiler_params=_cp(("parallel", "parallel")),
        name="mod_norm",
    )(xs, gain.reshape(1, d), modtab, modtab)


def _norm_router(xs, gain, modtab, k_shift, k_scale, wr_hi, wr_lo, rb):
    bsz, tlen, d = xs.shape
    blk = pl.BlockSpec((1, SEG_ROWS, d), lambda b, t: (b, t, 0))
    const = lambda b, t: (0, 0)
    return pl.pallas_call(
        _norm_router_kernel,
        grid=(bsz, tlen // SEG_ROWS),
        in_specs=_norm_specs(d, k_shift, k_scale) + [
            pl.BlockSpec((d, LANES), const), pl.BlockSpec((d, LANES), const),
            pl.BlockSpec((1, LANES), const)],
        out_specs=[blk, pl.BlockSpec((1, SEG_ROWS, LANES), lambda b, t: (b, t, 0))],
        out_shape=[jax.ShapeDtypeStruct((bsz, tlen, d), F32),
                   jax.ShapeDtypeStruct((bsz, tlen, LANES), F32)],
        compiler_params=_cp(("parallel", "parallel")),
        name="mod_norm_router",
    )(xs, gain.reshape(1, d), modtab, modtab, wr_hi, wr_lo, rb)


def _mm_kernel(a_ref, w_ref, o_ref):
    o_ref[...] = _dot(a_ref[...], w_ref[0]).astype(o_ref.dtype)


def _matmul(a, w_all, layer, out_dtype, tm=1024, tn=512):
    m, k = a.shape
    n = w_all.shape[2]
    return pl.pallas_call(
        _mm_kernel,
        grid=(m // tm, n // tn),
        in_specs=[pl.BlockSpec((tm, k), lambda i, j: (i, 0)),
                  pl.BlockSpec((1, k, tn), lambda i, j: (layer, 0, j))],
        out_specs=pl.BlockSpec((tm, tn), lambda i, j: (i, j)),
        out_shape=jax.ShapeDtypeStruct((m, n), out_dtype),
        compiler_params=_cp(("parallel", "parallel")),
        name="matmul",
    )(a, w_all)


def _seg_mod_row(block, blocks_per_batch):
    b = block // blocks_per_batch
    return 2 * b + jnp.minimum(block % blocks_per_batch, 1)


def _mm_resid_kernel(a_ref, w_ref, x_ref, mod_ref, o_ref, *, blocks_per_batch):
    acc = _dot(a_ref[...], w_ref[0])
    sub = a_ref.shape[0] // SEG_ROWS
    for s in range(sub):
        row = _seg_mod_row(pl.program_id(0) * sub + s, blocks_per_batch)
        rows = slice(s * SEG_ROWS, (s + 1) * SEG_ROWS)
        o_ref[rows, :] = x_ref[rows, :] + mod_ref[row] * acc[rows, :]


def _matmul_resid(a, w_all, layer, x, modtab, k_mod, blocks_per_batch, tm=1024, tn=512):
    m, k = a.shape
    n = w_all.shape[2]
    nrow = modtab.shape[0]
    return pl.pallas_call(
        functools.partial(_mm_resid_kernel, blocks_per_batch=blocks_per_batch),
        grid=(m // tm, n // tn),
        in_specs=[pl.BlockSpec((tm, k), lambda i, j: (i, 0)),
                  pl.BlockSpec((1, k, tn), lambda i, j: (layer, 0, j)),
                  pl.BlockSpec((tm, tn), lambda i, j: (i, j)),
                  pl.BlockSpec((nrow, 1, tn), lambda i, j: (0, 0, k_mod * (n // tn) + j))],
        out_specs=pl.BlockSpec((tm, tn), lambda i, j: (i, j)),
        out_shape=jax.ShapeDtypeStruct((m, n), F32),
        compiler_params=_cp(("parallel", "parallel")),
        name="matmul_resid",
    )(a, w_all, x, modtab)


def _merge_kernel(ya_ref, yb_ref, yc_ref, yd_ref, ga_ref, gb_ref, gc_ref, gd_ref, wb_ref, o_ref):
    acc = None
    for i, (y_ref, g_ref) in enumerate(((ya_ref, ga_ref), (yb_ref, gb_ref),
                                        (yc_ref, gc_ref), (yd_ref, gd_ref))):
        term = _sigmoid(g_ref[...].astype(F32)) * _dot(y_ref[...], wb_ref[0, i])
        acc = term if acc is None else acc + term
    o_ref[...] = acc.astype(o_ref.dtype)


def _merge(ys, gates, wb_all, layer, tm=1024, tn=512):
    m, bw = ys[0].shape
    n = wb_all.shape[3]
    nj = n // tn
    y_spec = pl.BlockSpec((tm, bw), lambda i, j: (i, 0))
    g_specs = [pl.BlockSpec((tm, tn), functools.partial(lambda i, j, q: (i, q * nj + j), q=q))
               for q in range(N_BRANCH)]
    return pl.pallas_call(
        _merge_kernel,
        grid=(m // tm, nj),
        in_specs=[y_spec] * N_BRANCH + g_specs + [
            pl.BlockSpec((1, N_BRANCH, bw, tn), lambda i, j: (layer, 0, 0, j))],
        out_specs=pl.BlockSpec((tm, tn), lambda i, j: (i, j)),
        out_shape=jax.ShapeDtypeStruct((m, n), BF16),
        compiler_params=_cp(("parallel", "parallel")),
        name="branch_merge",
    )(*ys, gates, gates, gates, gates, wb_all)


def _s5_pack_kernel(x_ref, o_ref, xf, ob):
    nb = x_ref.shape[0]
    cpb = S5_PACK_ROWS // S5_CHUNK
    ppt = LANES // S5_PIECE
    for j in range(BRANCH_W // LANES):
        xf[j] = x_ref[:, :, j * LANES:(j + 1) * LANES].astype(F32)
    for t in range(S5_CHUNK):
        for j in range(BRANCH_W // LANES):
            rows = xf[j, :, pl.ds(t, cpb, stride=S5_CHUNK), :].reshape(nb * cpb, LANES)
            for k in range(ppt):
                ob[j * ppt + k, :, t * S5_PIECE:(t + 1) * S5_PIECE] = rows[:, k * S5_PIECE:(k + 1) * S5_PIECE]
    o_ref[...] = ob[...].astype(o_ref.dtype)


def _s5_pack(p1_3d):
    nb, tlen, _ = p1_3d.shape
    rows = nb * (S5_PACK_ROWS // S5_CHUNK)
    return pl.pallas_call(
        _s5_pack_kernel,
        grid=(tlen // S5_PACK_ROWS,),
        in_specs=[pl.BlockSpec((nb, S5_PACK_ROWS, BRANCH_W), lambda i: (0, i, 0))],
        out_specs=pl.BlockSpec((S5_NPAIR, rows, BRANCH_W), lambda i: (0, i, 0)),
        out_shape=jax.ShapeDtypeStruct((S5_NPAIR, (tlen // S5_PACK_ROWS) * rows, BRANCH_W), BF16),
        scratch_shapes=[pltpu.VMEM((BRANCH_W // LANES, nb, S5_PACK_ROWS, LANES), F32),
                        pltpu.VMEM((S5_NPAIR, rows, BRANCH_W), F32)],
        compiler_params=_cp(("parallel",)),
        name="s5_pack",
    )(p1_3d)


def _s5_kernel(u_ref, ball_ref, toep_ref, call_ref, lam_ref, y_ref, st, *, nb, ctx_chunks):
    u = u_ref[0]
    s_all = _dot(u, ball_ref[0, 0])
    for k in range(4):
        st[k] = s_all[:, k * LANES:(k + 1) * LANES]
    cpb = S5_PACK_ROWS // S5_CHUNK
    nchunk = u.shape[0] // nb
    lam = lam_ref[0, 0]
    lfr, lfi, lbr, lbi = [jnp.broadcast_to(lam[:, k * LANES:(k + 1) * LANES], (nb, LANES)) for k in range(4)]

    def chunk_rows(c):
        return pl.ds((c // cpb) * (cpb * nb) + c % cpb, nb, stride=cpb)

    def step(rows, k, xr, xi, lr, li):
        s_r = st[2 * k, rows, :]
        s_i = st[2 * k + 1, rows, :]
        st[2 * k, rows, :] = xr
        st[2 * k + 1, rows, :] = xi
        return lr * xr - li * xi + s_r, lr * xi + li * xr + s_i

    def body(k, carry):
        fr, fi, br, bi = carry
        fr, fi = step(chunk_rows(k), 0, fr, fi, lfr, lfi)
        cb = jnp.where(k < ctx_chunks, ctx_chunks - 1 - k, nchunk - 1 - (k - ctx_chunks))
        br, bi = step(chunk_rows(cb), 1, br, bi, lbr, lbi)
        return fr, fi, br, bi

    zero = jnp.zeros((nb, LANES), F32)
    lax.fori_loop(0, nchunk, body, (zero, zero, zero, zero))
    states = jnp.concatenate([st[k] for k in range(4)], axis=-1).astype(BF16)
    y_ref[0] = _dot(u, toep_ref[0, 0]) + _dot_t(states, call_ref[0, 0])


def _s5_params(a_re, a_im, log_step, b_re, b_im, c_re, c_im):
    L, P, CH, NP = S5_CHUNK, S5_STATE, S5_CH, S5_NPAIR
    lam = lax.complex(a_re, a_im)
    lam_dt = lam * jnp.exp(log_step)[..., None]
    lam_bar = jnp.exp(lam_dt)
    b_bar = ((lam_bar - 1.0) / lam)[..., None] * lax.complex(b_re, b_im)
    c_mat = lax.complex(c_re, c_im)
    t = jnp.arange(L, dtype=F32)
    mid = float(L // 2)

    def power(z, exps):
        return jnp.exp(lam_dt[z][None] * exps[:, None, None].astype(jnp.complex64))

    def row_side(z, exps):
        return power(z, exps)[:, :, None, :] * jnp.swapaxes(b_bar[z], 1, 2)[None]

    def col_side(z, exps):
        return power(z, exps)[:, :, None, :] * c_mat[z][None]

    gi = lax.broadcasted_iota(jnp.int32, (NP, L, 2, CH, P), 2)

    def pair_mat(x):
        x5 = jnp.transpose(x.reshape(L, NP, 2, CH, P), (1, 0, 2, 3, 4))
        blocks = [jnp.where(gi == h, x5, 0.0) for h in range(2)]
        return jnp.concatenate(blocks, axis=-1).reshape(NP, L * S5_PIECE, 2 * P)

    def cplx_pair(x, conj_sign):
        return jnp.concatenate([pair_mat(x.real), pair_mat(conj_sign * x.imag)], axis=-1)

    hi = lax.Precision.HIGHEST
    fwd = jnp.einsum('qap,qbp->qab', cplx_pair(row_side(0, mid - t), 1.0),
                     cplx_pair(col_side(0, t - mid), -1.0), precision=hi)
    bwd = jnp.einsum('qap,qbp->qab', cplx_pair(row_side(1, t - mid), 1.0),
                     cplx_pair(col_side(1, mid - t), -1.0), precision=hi)
    s_of = lax.broadcasted_iota(jnp.int32, (L * S5_PIECE, L * S5_PIECE), 0) // S5_PIECE
    t_of = lax.broadcasted_iota(jnp.int32, (L * S5_PIECE, L * S5_PIECE), 1) // S5_PIECE
    toep = jnp.where(t_of >= s_of, fwd, 0.0) + jnp.where(s_of >= t_of, bwd, 0.0)

    ball = jnp.concatenate([cplx_pair(row_side(0, L - 1.0 - t), 1.0), cplx_pair(row_side(1, t), 1.0)], axis=-1)
    call_t = jnp.concatenate([cplx_pair(col_side(0, t + 1.0), -1.0), cplx_pair(col_side(1, L - t), -1.0)], axis=-1)

    lam_l = jnp.exp(lam_dt * float(L))
    lam_l = jnp.stack([lam_l[0].real, lam_l[0].imag, lam_l[1].real, lam_l[1].imag])
    lam_l = jnp.transpose(lam_l.reshape(4, NP, 2 * P), (1, 0, 2)).reshape(NP, 1, 8 * P)
    return ball.astype(BF16), toep.astype(BF16), call_t.astype(BF16), lam_l


def _s5_scan(lhs, prm_all, layer, nb):
    npair, rows, w = lhs.shape
    ball, toep, call, lam_l = prm_all
    spec_w = pl.BlockSpec((1, 1, w, w), lambda q: (layer, q, 0, 0))
    return pl.pallas_call(
        functools.partial(_s5_kernel, nb=nb, ctx_chunks=CTX_LEN // S5_CHUNK),
        grid=(npair,),
        in_specs=[pl.BlockSpec((1, rows, w), lambda q: (q, 0, 0)), spec_w, spec_w, spec_w,
                  pl.BlockSpec((1, 1, 1, w), lambda q: (layer, q, 0, 0))],
        out_specs=pl.BlockSpec((1, rows, w), lambda q: (q, 0, 0)),
        out_shape=jax.ShapeDtypeStruct((npair, rows, w), F32),
        scratch_shapes=[pltpu.VMEM((4, rows, LANES), F32)],
        compiler_params=_cp(("parallel",)),
        name="s5_scan",
    )(lhs, ball, toep, call, lam_l)


def _glu_kernel(y_ref, u_ref, d_ref, w_ref, b_ref, o_ref, yn):
    nb = u_ref.shape[0]
    cpb = S5_PACK_ROWS // S5_CHUNK
    ppt = LANES // S5_PIECE
    ntile = BRANCH_W // LANES
    for t in range(S5_CHUNK):
        for j in range(ntile):
            tile = jnp.concatenate([y_ref[j * ppt + k, :, t * S5_PIECE:(t + 1) * S5_PIECE]
                                    for k in range(ppt)], axis=-1)
            yn[j, :, pl.ds(t, cpb, stride=S5_CHUNK), :] = tile.reshape(nb, cpb, LANES)
    yd = jnp.concatenate([yn[j] for j in range(ntile)], axis=-1)
    y = d_ref[...] * u_ref[...].astype(F32).reshape(nb * S5_PACK_ROWS, BRANCH_W)
    y = y + yd.reshape(nb * S5_PACK_ROWS, BRANCH_W)
    cdf = 0.5 * (1.0 + jnp.tanh(math.sqrt(2.0 / math.pi) * (y + 0.044715 * (y * y * y))))
    g = y * cdf
    z = _dot(g.astype(BF16), w_ref[0]) + b_ref[...]
    o_ref[...] = (g * _sigmoid(z)).astype(o_ref.dtype).reshape(o_ref.shape)


def _s5_glu(y, p1_3d, d_skip, w_glu_all, layer, b_glu):
    nb, tlen, _ = p1_3d.shape
    bw = BRANCH_W
    rows = nb * (S5_PACK_ROWS // S5_CHUNK)
    const = lambda i: (0, 0)
    blk = pl.BlockSpec((nb, S5_PACK_ROWS, bw), lambda i: (0, i, 0))
    return pl.pallas_call(
        _glu_kernel,
        grid=(tlen // S5_PACK_ROWS,),
        in_specs=[pl.BlockSpec((S5_NPAIR, rows, bw), lambda i: (0, i, 0)), blk,
                  pl.BlockSpec((1, bw), const), pl.BlockSpec((1, bw, bw), lambda i: (layer, 0, 0)),
                  pl.BlockSpec((1, bw), const)],
        out_specs=blk,
        out_shape=jax.ShapeDtypeStruct((nb, tlen, bw), BF16),
        scratch_shapes=[pltpu.VMEM((bw // LANES, nb, S5_PACK_ROWS, LANES), F32)],
        compiler_params=_cp(("parallel",)),
        name="s5_glu",
    )(y, p1_3d, d_skip.reshape(1, bw), w_glu_all, b_glu.reshape(1, bw))


def _lane_masks(rows):
    lane = lax.broadcasted_iota(jnp.int32, (rows, LANES), 1)
    return lane < HEAD_DIM, (lane % HEAD_DIM) < (HEAD_DIM // 2)


def _head_norm(x, gain, lo):
    ss = x * x
    s_lo = jnp.sum(jnp.where(lo, ss, 0.0), axis=-1, keepdims=True)
    s_hi = jnp.sum(jnp.where(lo, 0.0, ss), axis=-1, keepdims=True)
    ms = jnp.where(lo, s_lo, s_hi) * (1.0 / HEAD_DIM)
    return x * lax.rsqrt(ms + EPS) * gain


def _rope(x, cos, sin_signed, first_half):
    partner = jnp.where(first_half, pltpu.roll(x, LANES - HEAD_DIM // 2, 1),
                        pltpu.roll(x, HEAD_DIM // 2, 1))
    return x * cos + partner * sin_signed


def _prep_slab(x, gain, cos, sin_signed, scale):
    rows = x.shape[0]
    lo, first_half = _lane_masks(rows)
    y = _head_norm(x.astype(F32), gain, lo)
    if cos is not None:
        y = _rope(y, cos, sin_signed, first_half)
    if scale is not None:
        y = y * scale
    return y.astype(BF16)


def _prep_keys(k_ref, kn_ref, gain, cos_ref, sin_ref, nslab):
    tlen = k_ref.shape[1]
    for r0 in range(0, tlen, SEG_ROWS):
        rows = slice(r0, r0 + SEG_ROWS)
        for s in range(nslab):
            cols = slice(s * LANES, (s + 1) * LANES)
            cos = None if cos_ref is None else cos_ref[rows, :]
            sin = None if sin_ref is None else sin_ref[rows, :]
            kn_ref[rows, cols] = _prep_slab(k_ref[0, rows, cols], gain, cos, sin, None)


def _prep_values(v_ref, va_ref, nslab):
    tlen = v_ref.shape[1]
    for r0 in range(0, tlen, SEG_ROWS):
        rows = slice(r0, r0 + SEG_ROWS)
        lo, _ = _lane_masks(SEG_ROWS)
        for s in range(nslab):
            cols = slice(s * LANES, (s + 1) * LANES)
            v = v_ref[0, rows, cols]
            one = jnp.ones_like(v)
            va_ref[0, rows, cols] = jnp.where(lo, v, one)
            va_ref[1, rows, cols] = jnp.where(lo, one, v)


def _split_heads(q):
    lo, _ = _lane_masks(q.shape[0])
    zero = jnp.zeros_like(q)
    return jnp.where(lo, q, zero), jnp.where(lo, zero, q), lo


def _join_heads(o_lo, o_hi, extra_lo, extra_hi, lo):
    num = jnp.where(lo, o_lo, o_hi)
    den = pltpu.roll(jnp.where(lo, o_hi, o_lo), HEAD_DIM, 1)
    if extra_lo is not None:
        den = den + jnp.where(lo, extra_lo, extra_hi)
    return num / den


def _win_kernel(sink_ref, q_ref, k_ref, v_ref, cos_ref, sin_ref, qg_ref, kg_ref, o_ref,
                kn_ref, va_ref, qs_ref, e_ref):
    qi = pl.program_id(1)
    tq = q_ref.shape[1]
    tlen = k_ref.shape[1]
    span = tq + 2 * WINDOW
    nhead = WIN_HEADS

    @pl.when(qi == 0)
    def _():
        _prep_keys(k_ref, kn_ref, kg_ref[...], cos_ref, sin_ref, 1)
        _prep_values(v_ref, va_ref, 1)

    qrows = pl.ds(pl.multiple_of(qi * tq, tq), tq)
    cos_q = cos_ref[qrows, :]
    sin_q = sin_ref[qrows, :]

    def attend(local):
        for s in range(nhead // 2):
            cols = slice(s * LANES, (s + 1) * LANES)
            qn = _prep_slab(q_ref[0, :, cols], qg_ref[...], cos_q, sin_q, SCALE * LOG2E)
            q_lo, q_hi, _ = _split_heads(qn)
            qs_ref[(2 * s) * tq:(2 * s + 1) * tq, :] = q_lo
            qs_ref[(2 * s + 1) * tq:(2 * s + 2) * tq, :] = q_hi
        qs = qs_ref[...]
        s_ctx = _dot_t(qs, kn_ref[0:CTX_LEN, :])
        m = jnp.max(s_ctx, axis=-1, keepdims=True)
        if local:
            start = jnp.clip(qi * tq - WINDOW, CTX_LEN, tlen - span)
            krows = pl.ds(pl.multiple_of(start, LANES), span)
            qtok = qi * tq + lax.broadcasted_iota(jnp.int32, (tq, span), 0)
            ktok = start + lax.broadcasted_iota(jnp.int32, (tq, span), 1)
            valid = (jnp.abs(qtok - ktok) <= WINDOW)[None]
            s_loc = _dot_t(qs, kn_ref[krows, :]).reshape(nhead, tq, span)
            s_loc = jnp.where(valid, s_loc, NEG_INF).reshape(nhead * tq, span)
            m = jnp.maximum(m, jnp.max(s_loc, axis=-1, keepdims=True))
        sinks = [sink_ref[(hb // 2) + (hb % 2) * (nhead // 2)] * LOG2E for hb in range(nhead)]
        m = jnp.concatenate([jnp.maximum(m[hb * tq:(hb + 1) * tq], sinks[hb]) for hb in range(nhead)], axis=0)
        e_ref[:, 0:CTX_LEN] = jnp.exp2(s_ctx - m).astype(BF16)
        if local:
            e_ref[:, CTX_LEN:CTX_LEN + span] = jnp.exp2(s_loc - m).astype(BF16)
        lo, _ = _lane_masks(tq)
        for s in range(nhead // 2):
            outs, extra = [], []
            for half in range(2):
                hb = 2 * s + half
                rows = slice(hb * tq, (hb + 1) * tq)
                o = _dot(e_ref[rows, 0:CTX_LEN], va_ref[half, 0:CTX_LEN, :])
                if local:
                    o = o + _dot(e_ref[rows, CTX_LEN:CTX_LEN + span], va_ref[half, krows, :])
                outs.append(o)
                extra.append(jnp.exp2(sinks[hb] - m[rows]))
            o_ref[0, :, s * LANES:(s + 1) * LANES] = _join_heads(
                outs[0], outs[1], extra[0], extra[1], lo).astype(o_ref.dtype)

    @pl.when(qi < CTX_LEN // tq)
    def _():
        attend(False)

    @pl.when(qi >= CTX_LEN // tq)
    def _():
        attend(True)


def _win_attn(p1, sink, cos_t, sin_t, qg, kg, tq=128):
    bsz, tlen, _ = p1.shape
    kcol = 8 * BRANCH_W // LANES
    return pl.pallas_call(
        _win_kernel,
        grid=(bsz, tlen // tq),
        in_specs=[pl.BlockSpec(memory_space=pltpu.SMEM),
                  pl.BlockSpec((1, tq, BRANCH_W), lambda b, i: (b, i, 1)),
                  pl.BlockSpec((1, tlen, LANES), lambda b, i: (b, 0, kcol)),
                  pl.BlockSpec((1, tlen, LANES), lambda b, i: (b, 0, kcol + 1)),
                  pl.BlockSpec((tlen, LANES), lambda b, i: (0, 0)),
                  pl.BlockSpec((tlen, LANES), lambda b, i: (0, 0)),
                  pl.BlockSpec((1, LANES), lambda b, i: (0, 0)),
                  pl.BlockSpec((1, LANES), lambda b, i: (0, 0))],
        out_specs=pl.BlockSpec((1, tq, BRANCH_W), lambda b, i: (b, i, 0)),
        out_shape=jax.ShapeDtypeStruct((bsz, tlen, BRANCH_W), BF16),
        scratch_shapes=[pltpu.VMEM((tlen, LANES), BF16), pltpu.VMEM((2, tlen, LANES), BF16),
                        pltpu.VMEM((WIN_HEADS * tq, LANES), BF16),
                        pltpu.VMEM((WIN_HEADS * tq, CTX_LEN + tq + 2 * WINDOW), BF16)],
        compiler_params=_cp(("parallel", "arbitrary")),
        name="win_attn",
    )(sink, p1, p1, p1, cos_t, sin_t, qg, kg)


def _diff_kernel(lam_ref, q_ref, k_ref, v_ref, cos_ref, sin_ref, qg_ref, kg_ref, sg_ref, o_ref, kn_ref):
    qi = pl.program_id(1)
    tq = q_ref.shape[1]
    tlen = k_ref.shape[1]
    lam = lam_ref[0]
    out_scale = lam_ref[1]

    @pl.when(qi == 0)
    def _():
        _prep_keys(k_ref, kn_ref, kg_ref[...], cos_ref, sin_ref, DIFF_HEADS)

    qrows = pl.ds(pl.multiple_of(qi * tq, tq), tq)
    cos_q = cos_ref[qrows, :]
    sin_q = sin_ref[qrows, :]

    def attend(nk):
        for h in range(DIFF_HEADS):
            cols = slice(h * LANES, (h + 1) * LANES)
            qn = _prep_slab(q_ref[0, :, cols], qg_ref[...], cos_q, sin_q, SCALE * LOG2E)
            q1, q2, _ = _split_heads(qn)
            keys = kn_ref[0:nk, cols]
            vals = v_ref[0, 0:nk, cols]
            s1 = _dot_t(q1, keys)
            s2 = _dot_t(q2, keys)
            e1 = jnp.exp2(s1 - jnp.max(s1, axis=-1, keepdims=True))
            e2 = jnp.exp2(s2 - jnp.max(s2, axis=-1, keepdims=True))
            r1 = 1.0 / _rowsum(e1)
            r2 = lam / _rowsum(e2)
            o = _dot(e1.astype(BF16), vals) * r1 - _dot(e2.astype(BF16), vals) * r2
            o = o * lax.rsqrt(jnp.mean(o * o, axis=-1, keepdims=True) + EPS) * sg_ref[...]
            o_ref[0, :, cols] = (o * out_scale).astype(o_ref.dtype)

    @pl.when(qi < CTX_LEN // tq)
    def _():
        attend(CTX_LEN)

    @pl.when(qi >= CTX_LEN // tq)
    def _():
        attend(tlen)


def _diff_attn(p1, lam_vec, cos_t, sin_t, qg, kg, sg, tq=256):
    bsz, tlen, _ = p1.shape
    full = lambda c: pl.BlockSpec((1, tlen, BRANCH_W), functools.partial(lambda b, i, c: (b, 0, c), c=c))
    const = lambda b, i: (0, 0)
    return pl.pallas_call(
        _diff_kernel,
        grid=(bsz, tlen // tq),
        in_specs=[pl.BlockSpec(memory_space=pltpu.SMEM),
                  pl.BlockSpec((1, tq, BRANCH_W), lambda b, i: (b, i, 2)),
                  full(3), full(4),
                  pl.BlockSpec((tlen, LANES), const), pl.BlockSpec((tlen, LANES), const),
                  pl.BlockSpec((1, LANES), const), pl.BlockSpec((1, LANES), const),
                  pl.BlockSpec((1, LANES), const)],
        out_specs=pl.BlockSpec((1, tq, BRANCH_W), lambda b, i: (b, i, 0)),
        out_shape=jax.ShapeDtypeStruct((bsz, tlen, BRANCH_W), BF16),
        scratch_shapes=[pltpu.VMEM((tlen, BRANCH_W), BF16)],
        compiler_params=_cp(("parallel", "arbitrary")),
        name="diff_attn",
    )(lam_vec, p1, p1, p1, cos_t, sin_t, qg, kg, sg)


def _na_kernel(q_ref, k_ref, v_ref, bias_refs, qg_ref, kg_ref, o_ref, kn_ref, va_ref, qs_ref, sc_ref, e_ref,
               *, grid_rows):
    qi = pl.program_id(1)
    span = NA_WIN_R * GRID_W
    nslab = NA_HEADS // 2

    @pl.when(qi == 0)
    def _():
        _prep_keys(k_ref, kn_ref, kg_ref[...], None, None, nslab)
        _prep_values(v_ref, va_ref, nslab)

    def attend(qrows, nq, krows, bias_ref=None):
        for s in range(nslab):
            cols = slice(s * LANES, (s + 1) * LANES)
            qn = _prep_slab(q_ref[0, qrows, cols], qg_ref[...], None, None, SCALE * LOG2E)
            q_lo, q_hi, _ = _split_heads(qn)
            qs_ref[(2 * s) * nq:(2 * s + 1) * nq, :] = q_lo
            qs_ref[(2 * s + 1) * nq:(2 * s + 2) * nq, :] = q_hi
        ncol = CTX_LEN if krows is None else CTX_LEN + span
        for s in range(nslab):
            cols = slice(s * LANES, (s + 1) * LANES)
            rows = slice(2 * s * nq, (2 * s + 2) * nq)
            sc_ref[rows, 0:CTX_LEN] = _dot_t(qs_ref[rows, :], kn_ref[0:CTX_LEN, cols])
            if krows is not None:
                bias = bias_ref[0, 2 * s:2 * s + 2].reshape(2 * nq, span)
                sc_ref[rows, CTX_LEN:ncol] = _dot_t(qs_ref[rows, :], kn_ref[krows, cols]) + bias
        nrow = NA_HEADS * nq
        sc = sc_ref[0:nrow, 0:ncol]
        e_ref[0:nrow, 0:ncol] = jnp.exp2(sc - jnp.max(sc, axis=-1, keepdims=True)).astype(BF16)
        lo, _ = _lane_masks(nq)
        for s in range(nslab):
            cols = slice(s * LANES, (s + 1) * LANES)
            outs = []
            for half in range(2):
                rows = slice((2 * s + half) * nq, (2 * s + half + 1) * nq)
                o = _dot(e_ref[rows, 0:CTX_LEN], va_ref[half, 0:CTX_LEN, cols])
                if krows is not None:
                    o = o + _dot(e_ref[rows, CTX_LEN:ncol], va_ref[half, krows, cols])
                outs.append(o)
            o_ref[0, qrows, cols] = _join_heads(outs[0], outs[1], None, None, lo).astype(o_ref.dtype)

    @pl.when(qi == 0)
    def _():
        attend(slice(0, CTX_LEN), CTX_LEN, None)

    @pl.when(qi > 0)
    def _():
        for k, bias_ref in enumerate(bias_refs):
            r = len(bias_refs) * (qi - 1) + k
            qrows = pl.ds(pl.multiple_of(CTX_LEN + r * GRID_W, GRID_W), GRID_W)
            k0 = jnp.clip(r - NA_WIN_R // 2, 0, grid_rows - NA_WIN_R)
            krows = pl.ds(pl.multiple_of(CTX_LEN + k0 * GRID_W, GRID_W), span)
            attend(qrows, GRID_W, krows, bias_ref)


def _na_bias(rpb_all, rows):
    win_r = NA_WIN_R
    half = win_r // 2
    r = jnp.concatenate([jnp.arange(half + 1), jnp.arange(rows - half + 1, rows)])
    row_idx = jnp.clip(r - half, 0, rows - win_r)[:, None] + jnp.arange(win_r)[None, :]
    r_off = row_idx - r[:, None] + (NA_WIN_R - 1)
    col = jnp.arange(GRID_W)
    c_off = jnp.clip(col[None, :] - col[:, None] + (NA_WIN_C - 1), 0, 2 * NA_WIN_C - 2)
    onehot = (c_off[None] == jnp.arange(2 * NA_WIN_C - 1)[:, None, None]).astype(F32)
    rows_sel = rpb_all.astype(F32)[:, :, r_off] * LOG2E
    bias = jnp.einsum('lhpwk,kqc->lphqwc', rows_sel, onehot, precision=lax.Precision.HIGHEST)
    col_start = jnp.clip(col - NA_WIN_C // 2, 0, GRID_W - NA_WIN_C)
    col_ok = (col[None, :] >= col_start[:, None]) & (col[None, :] < col_start[:, None] + NA_WIN_C)
    bias = jnp.where(col_ok[None, None, None, :, None, :], bias, NEG_INF)
    depth = rpb_all.shape[0]
    return bias.reshape(depth, r.shape[0], NA_HEADS, GRID_W, win_r * GRID_W)


def _na_attn(p1, bias_all, layer, qg, kg):
    bsz, tlen, _ = p1.shape
    rows = (tlen - CTX_LEN) // GRID_W
    half = NA_WIN_R // 2
    full = lambda c: pl.BlockSpec((1, tlen, BRANCH_W), functools.partial(lambda b, i, c: (b, 0, c), c=c))
    const = lambda b, i: (0, 0)

    def bias_spec(k):
        def bias_idx(b, i):
            r = jnp.maximum(NA_ROWS_PER_STEP * (i - 1) + k, 0)
            return (layer, jnp.minimum(r, half) + jnp.maximum(r - (rows - half), 0), 0, 0, 0)
        return pl.BlockSpec((1, 1, NA_HEADS, GRID_W, NA_WIN_R * GRID_W), bias_idx)

    return pl.pallas_call(
        functools.partial(_na_kernel_wrap, grid_rows=rows),
        grid=(bsz, rows // NA_ROWS_PER_STEP + 1),
        in_specs=[full(5), full(6), full(7)] + [bias_spec(k) for k in range(NA_ROWS_PER_STEP)] + [
                  pl.BlockSpec((1, LANES), const), pl.BlockSpec((1, LANES), const)],
        out_specs=pl.BlockSpec((1, tlen, BRANCH_W), lambda b, i: (b, 0, 0)),
        out_shape=jax.ShapeDtypeStruct((bsz, tlen, BRANCH_W), BF16),
        scratch_shapes=[pltpu.VMEM((tlen, BRANCH_W), BF16), pltpu.VMEM((2, tlen, BRANCH_W), BF16),
                        pltpu.VMEM((NA_HEADS * CTX_LEN, LANES), BF16),
                        pltpu.VMEM((NA_HEADS * CTX_LEN, CTX_LEN + NA_WIN_R * GRID_W), F32),
                        pltpu.VMEM((NA_HEADS * CTX_LEN, CTX_LEN + NA_WIN_R * GRID_W), BF16)],
        compiler_params=_cp(("parallel", "arbitrary")),
        name="na_attn",
    )(p1, p1, p1, *([bias_all] * NA_ROWS_PER_STEP), qg, kg)


def _na_kernel_wrap(q_ref, k_ref, v_ref, *rest, grid_rows):
    bias_refs = [r.at[0] for r in rest[:NA_ROWS_PER_STEP]]
    _na_kernel(q_ref, k_ref, v_ref, bias_refs, *rest[NA_ROWS_PER_STEP:], grid_rows=grid_rows)


def _row_copy(src_hbm, row, dst, r, sem):
    return pltpu.make_async_copy(src_hbm.at[pl.ds(row, 1), :], dst.at[pl.ds(r, 1), :], sem)


def _gather_rows(idx_ref, base, src_hbm, dst, sem, nrows):
    for r in range(nrows):
        _row_copy(src_hbm, idx_ref[base + r], dst, r, sem).start()


def _gather_wait(src_hbm, dst, sem, nrows):
    pltpu.make_async_copy(src_hbm.at[pl.ds(0, nrows), :], dst, sem).wait()


def _expert_kernel(te_ref, rt_ref, na_ref, h_hbm, rw_ref, w1_ref, w3_ref, w2_ref, o_ref, buf, sem):
    i = pl.program_id(0)
    tm = buf.shape[1]
    slot = i % 2
    nact = jnp.maximum(na_ref[0], 1)

    @pl.when(i == 0)
    def _():
        _gather_rows(rt_ref, 0, h_hbm, buf.at[0], sem.at[0], tm)

    @pl.when(i < nact)
    def _():
        _gather_wait(h_hbm, buf.at[slot], sem.at[slot], tm)
        nxt = jnp.minimum(i + 1, pl.num_programs(0) - 1)
        _gather_rows(rt_ref, nxt * tm, h_hbm, buf.at[1 - slot], sem.at[1 - slot], tm)
        x = buf[slot].astype(BF16)
        a = _dot(x, w1_ref[0, 0])
        mid = (a * jax.nn.sigmoid(a)) * _dot(x, w3_ref[0, 0])
        o_ref[...] = _dot(mid.astype(BF16), w2_ref[0, 0]) * rw_ref[...]

    @pl.when(i == nact - 1)
    def _():
        _gather_wait(h_hbm, buf.at[1 - slot], sem.at[1 - slot], tm)

    @pl.when(i >= nact)
    def _():
        o_ref[...] = jnp.zeros_like(o_ref)


def _moe_experts(h, tile_expert, row_token, n_active, row_weight, w1_all, w3_all, w2_all, layer, tm=MOE_TM):
    n, d = h.shape
    rmax = row_token.shape[0]
    ff = w1_all.shape[3]
    grid_spec = pltpu.PrefetchScalarGridSpec(
        num_scalar_prefetch=3,
        grid=(rmax // tm,),
        in_specs=[pl.BlockSpec(memory_space=pl.ANY),
                  pl.BlockSpec((tm, 1), lambda i, te, rt, na: (i, 0)),
                  pl.BlockSpec((1, 1, d, ff), lambda i, te, rt, na: (layer, te[i], 0, 0)),
                  pl.BlockSpec((1, 1, d, ff), lambda i, te, rt, na: (layer, te[i], 0, 0)),
                  pl.BlockSpec((1, 1, ff, d), lambda i, te, rt, na: (layer, te[i], 0, 0))],
        out_specs=pl.BlockSpec((tm, d), lambda i, te, rt, na: (i, 0)),
        scratch_shapes=[pltpu.VMEM((2, tm, d), F32), pltpu.SemaphoreType.DMA((2,))])
    return pl.pallas_call(
        _expert_kernel,
        grid_spec=grid_spec,
        out_shape=jax.ShapeDtypeStruct((rmax, d), F32),
        compiler_params=_cp(("arbitrary",)),
        name="moe_experts",
    )(tile_expert, row_token, n_active, h, row_weight, w1_all, w3_all, w2_all)


def _combine_kernel(d0_ref, d1_ref, ys_hbm, x_ref, mod_ref, *rest, blocks_per_batch, next_norm):
    if next_norm:
        g_ref, sh_ref, sc_ref, o_ref, h_ref, buf0, buf1, sem = rest
    else:
        o_ref, buf0, buf1, sem = rest
    i = pl.program_id(0)
    n = pl.num_programs(0)
    tm = x_ref.shape[0]
    slot = i % 2

    def issue(tile, s):
        _gather_rows(d0_ref, tile * tm, ys_hbm, buf0.at[s], sem.at[0, s], tm)
        _gather_rows(d1_ref, tile * tm, ys_hbm, buf1.at[s], sem.at[1, s], tm)

    @pl.when(i == 0)
    def _():
        issue(0, 0)

    @pl.when(i + 1 < n)
    def _():
        issue(i + 1, 1 - slot)

    _gather_wait(ys_hbm, buf0.at[slot], sem.at[0, slot], tm)
    _gather_wait(ys_hbm, buf1.at[slot], sem.at[1, slot], tm)
    row = _seg_mod_row(i, blocks_per_batch)
    x_new = x_ref[...] + mod_ref[row] * (buf0[slot] + buf1[slot])
    o_ref[...] = x_new
    if next_norm:
        h_ref[...] = _mod_norm(x_new, g_ref[...], sh_ref[row], sc_ref[row]).astype(h_ref.dtype)


def _moe_combine(ys, dest0, dest1, x, modtab, k_mod, blocks_per_batch, next_norm=None):
    n, d = x.shape
    tm = SEG_ROWS
    nrow = modtab.shape[0]
    row_blk = pl.BlockSpec((tm, d), lambda i, a, b: (i, 0))
    mod_spec = lambda k: pl.BlockSpec((nrow, 1, d), lambda i, a, b: (0, 0, k))
    in_specs = [pl.BlockSpec(memory_space=pl.ANY), row_blk, mod_spec(k_mod)]
    args = [dest0, dest1, ys, x, modtab]
    out_specs = row_blk
    out_shape = jax.ShapeDtypeStruct((n, d), F32)
    if next_norm is not None:
        gain, modtab_next, k_shift, k_scale = next_norm
        in_specs += [pl.BlockSpec((1, d), lambda i, a, b: (0, 0)), mod_spec(k_shift), mod_spec(k_scale)]
        args += [gain.reshape(1, d), modtab_next, modtab_next]
        out_specs = [row_blk, row_blk]
        out_shape = [out_shape, jax.ShapeDtypeStruct((n, d), BF16)]
    grid_spec = pltpu.PrefetchScalarGridSpec(
        num_scalar_prefetch=2,
        grid=(n // tm,),
        in_specs=in_specs,
        out_specs=out_specs,
        scratch_shapes=[pltpu.VMEM((2, tm, d), F32), pltpu.VMEM((2, tm, d), F32),
                        pltpu.SemaphoreType.DMA((2, 2))])
    return pl.pallas_call(
        functools.partial(_combine_kernel, blocks_per_batch=blocks_per_batch, next_norm=next_norm is not None),
        grid_spec=grid_spec,
        out_shape=out_shape,
        compiler_params=_cp(("arbitrary",)),
        name="moe_combine",
    )(*args)


def _route(logits, tm=MOE_TM):
    n = logits.shape[0]
    g_prob = jax.nn.softmax(logits[:, :N_GROUPS], axis=-1)
    g_idx = jnp.argmax(g_prob, axis=-1).astype(jnp.int32)
    g_w = jnp.max(g_prob, axis=-1)
    e_logits = logits[:, N_GROUPS:N_GROUPS + N_EXPERTS].reshape(n, N_GROUPS, EXPERTS_PER_GROUP)
    in_group = jnp.arange(N_GROUPS, dtype=jnp.int32)[None, :, None] == g_idx[:, None, None]
    e_in = jnp.sum(jnp.where(in_group, e_logits, 0.0), axis=1)
    slot = jnp.arange(EXPERTS_PER_GROUP, dtype=jnp.int32)[None, :]
    i1 = jnp.argmax(e_in, axis=-1).astype(jnp.int32)
    v1 = jnp.max(e_in, axis=-1)
    rest = jnp.where(slot == i1[:, None], -jnp.inf, e_in)
    i2 = jnp.argmax(rest, axis=-1).astype(jnp.int32)
    v2 = jnp.max(rest, axis=-1)
    w_sel = jax.nn.softmax(jnp.stack([v1, v2], axis=-1), axis=-1) * g_w[:, None]
    expert = g_idx[:, None] * EXPERTS_PER_GROUP + jnp.stack([i1, i2], axis=-1)

    e_flat = expert.reshape(-1)
    w_flat = w_sel.reshape(-1)
    ids = jnp.arange(N_EXPERTS, dtype=jnp.int32)
    onehot = (e_flat[:, None] == ids[None, :]).astype(jnp.int32)
    csum = jnp.cumsum(onehot, axis=0)
    rank = jnp.sum(onehot * csum, axis=1) - 1
    counts = csum[-1]
    padded = ((counts + tm - 1) // tm) * tm
    pend = jnp.cumsum(padded)
    pstart = pend - padded
    ustart = jnp.cumsum(counts) - counts
    dest = (jnp.sum(onehot * pstart[None, :], axis=1) + rank).astype(jnp.int32)

    rmax = 2 * n + N_EXPERTS * tm
    total = pend[-1]
    tile_start = jnp.arange(rmax // tm, dtype=jnp.int32) * tm
    tile_expert = jnp.minimum(jnp.sum((tile_start[:, None] >= pend[None, :]).astype(jnp.int32), axis=1),
                              N_EXPERTS - 1)
    order = jnp.argsort(e_flat, stable=True).astype(jnp.int32)
    r = jnp.arange(rmax, dtype=jnp.int32)
    e_r = jnp.minimum(jnp.sum((r[:, None] >= pend[None, :]).astype(jnp.int32), axis=1), N_EXPERTS - 1)
    oh_r = (e_r[:, None] == ids[None, :]).astype(jnp.int32)
    off = r - jnp.sum(oh_r * pstart[None, :], axis=1)
    valid = (r < total) & (off < jnp.sum(oh_r * counts[None, :], axis=1))
    src = order[jnp.clip(jnp.sum(oh_r * ustart[None, :], axis=1) + off, 0, 2 * n - 1)]
    row_token = jnp.where(valid, src // 2, 0).astype(jnp.int32)
    row_weight = jnp.where(valid, w_flat[src], 0.0)
    n_active = (total // tm).astype(jnp.int32)
    last_expert = tile_expert[jnp.maximum(n_active - 1, 0)]
    tile_expert = jnp.where(tile_start < total, tile_expert, last_expert).astype(jnp.int32)
    dest2 = dest.reshape(n, 2)
    return (tile_expert, row_token, n_active.reshape(1), row_weight.reshape(rmax, 1),
            dest2[:, 0], dest2[:, 1])


def _rope_tables(n_lat):
    t = jnp.arange(n_lat, dtype=jnp.int32)
    row = (t // GRID_W).astype(F32)
    col = (t % GRID_W).astype(F32)
    per_axis = HEAD_DIM // 4
    inv_freq = ROPE_BASE ** (-jnp.arange(per_axis, dtype=F32) / per_axis)
    ang = jnp.concatenate([row[:, None] * inv_freq, col[:, None] * inv_freq], axis=-1)
    cos = jnp.concatenate([jnp.ones((CTX_LEN, HEAD_DIM // 2), F32), jnp.cos(ang)], axis=0)
    sin = jnp.concatenate([jnp.zeros((CTX_LEN, HEAD_DIM // 2), F32), jnp.sin(ang)], axis=0)
    cos_t = jnp.tile(cos, (1, 4))
    sin_t = jnp.tile(jnp.concatenate([-sin, sin], axis=-1), (1, 2))
    return cos_t, sin_t


def _win_head_perm():
    cols = []
    for s in range(WIN_HEADS // 2):
        for half in range(2):
            h = s + half * (WIN_HEADS // 2)
            cols.extend(range(h * HEAD_DIM, (h + 1) * HEAD_DIM))
    return cols


def _p1_weight(w_in):
    bw = BRANCH_W
    kvw = WIN_KV_HEADS * HEAD_DIM
    hd = HEAD_DIM
    cols = lambda a, b: w_in[:, :, a:b].astype(BF16)
    q_b = [cols(bw + h * hd, bw + (h + 1) * hd) for s in range(WIN_HEADS // 2) for h in (s, s + WIN_HEADS // 2)]
    kv0 = 2 * bw
    rest0 = kv0 + 2 * kvw
    pad = jnp.zeros(w_in.shape[:2] + (bw - 2 * kvw,), BF16)
    p1 = jnp.concatenate([cols(0, bw)] + q_b + [cols(rest0, rest0 + 6 * bw), cols(kv0, rest0), pad], axis=2)
    return p1, cols(rest0 + 6 * bw, w_in.shape[2])


def kernel(x, c, ctx, c_ctx, w_ada, b_ada, norm_mix, norm_ffn, w_in, s5_a_re, s5_a_im, s5_log_step, s5_b_re, s5_b_im, s5_c_re, s5_c_im, s5_d, s5_w_glu, s5_b_glu, win_qn, win_kn, win_sink, diff_qn, diff_kn, diff_lambda, diff_subln, na_qn, na_kn, na_rpb, w_branch, w_out, moe_w_group, moe_b_group, moe_w_expert, moe_b_expert, moe_w1, moe_w3, moe_w2):
    bsz, n_lat, d = x.shape
    tlen = CTX_LEN + n_lat
    ntok = bsz * tlen
    bpb = tlen // SEG_ROWS
    depth = w_ada.shape[0]

    cond = jnp.zeros((ADA_ROWS, d), F32).at[:bsz].set(c).at[bsz].set(c_ctx)
    mod = _ada_mod(cond, w_ada, b_ada)
    mod_ctx = jnp.broadcast_to(mod[:, bsz:bsz + 1], (depth, bsz, 6 * d))
    modtab = jnp.stack([mod_ctx, mod[:, :bsz]], axis=2).reshape(depth, 2 * bsz, 1, 6 * d)

    w_p1, w_gate = _p1_weight(w_in)
    wb_all = w_branch.astype(BF16)
    wb_win = jnp.concatenate([wb_all[:, 1:2, h * HEAD_DIM:(h + 1) * HEAD_DIM]
                              for s in range(WIN_HEADS // 2) for h in (s, s + WIN_HEADS // 2)], axis=2)
    wb_all = jnp.concatenate([wb_all[:, :1], wb_win, wb_all[:, 2:]], axis=1)
    w_out_bf = w_out.astype(BF16)
    w_glu_bf = s5_w_glu.astype(BF16)
    w1_bf, w3_bf, w2_bf = moe_w1.astype(BF16), moe_w3.astype(BF16), moe_w2.astype(BF16)
    w_r = jnp.zeros((depth, d, LANES), F32).at[:, :, :N_GROUPS].set(moe_w_group)
    w_r = w_r.at[:, :, N_GROUPS:N_GROUPS + N_EXPERTS].set(moe_w_expert)
    rb = jnp.zeros((depth, 1, LANES), F32).at[:, 0, :N_GROUPS].set(moe_b_group)
    rb = rb.at[:, 0, N_GROUPS:N_GROUPS + N_EXPERTS].set(moe_b_expert)
    wr_hi, wr_lo = _split_bf16(w_r)
    na_bias = _na_bias(na_rpb, n_lat // GRID_W)
    s5_prm = jax.vmap(_s5_params)(s5_a_re, s5_a_im, s5_log_step, s5_b_re, s5_b_im, s5_c_re, s5_c_im)

    cos_t, sin_t = _rope_tables(n_lat)
    tile2 = lambda g: jnp.tile(g.astype(F32), 2).reshape(1, LANES)
    xs = jnp.concatenate([ctx, x], axis=1)

    for l in range(depth):
        lam_init = 0.8 - 0.6 * math.exp(-0.3 * l)
        mt = modtab[l]

        if l == 0:
            h = _norm(xs, norm_mix[l], mt, 0, 1).reshape(ntok, d)
        p1 = _matmul(h, w_p1, l, BF16, tn=P1_WIDTH // 3)
        gates = _matmul(h, w_gate, l, BF16, tn=1024)
        p1_3d = p1.reshape(bsz, tlen, P1_WIDTH)

        y_s5 = _s5_scan(_s5_pack(p1_3d), s5_prm, l, bsz)
        y_a = _s5_glu(y_s5, p1_3d, s5_d[l], w_glu_bf, l, s5_b_glu[l]).reshape(ntok, BRANCH_W)

        y_b = _win_attn(p1_3d, win_sink[l].astype(F32), cos_t, sin_t,
                        tile2(win_qn[l]), tile2(win_kn[l])).reshape(ntok, BRANCH_W)

        lp = diff_lambda[l].astype(F32)
        lam = jnp.exp(jnp.sum(lp[0] * lp[1])) - jnp.exp(jnp.sum(lp[2] * lp[3])) + lam_init
        lam_vec = jnp.stack([lam, jnp.asarray(1.0 - lam_init, F32)])
        y_c = _diff_attn(p1_3d, lam_vec, cos_t, sin_t, tile2(diff_qn[l]), tile2(diff_kn[l]),
                         diff_subln[l].astype(F32).reshape(1, LANES)).reshape(ntok, BRANCH_W)

        y_d = _na_attn(p1_3d, na_bias, l, tile2(na_qn[l]), tile2(na_kn[l])).reshape(ntok, BRANCH_W)

        merged = _merge((y_a, y_b, y_c, y_d), gates, wb_all, l)
        xs = _matmul_resid(merged, w_out_bf, l, xs.reshape(ntok, d), mt, 2, bpb, tn=1024)
        xs = xs.reshape(bsz, tlen, d)

        hf, logits = _norm_router(xs, norm_ffn[l], mt, 3, 4, wr_hi[l], wr_lo[l], rb[l])
        te, rt, na, rw, d0, d1 = _route(logits.reshape(ntok, LANES))
        ys = _moe_experts(hf.reshape(ntok, d), te, rt, na, rw, w1_bf, w3_bf, w2_bf, l)
        if l + 1 < depth:
            xs, h = _moe_combine(ys, d0, d1, xs.reshape(ntok, d), mt, 5, bpb,
                                 next_norm=(norm_mix[l + 1], modtab[l + 1], 0, 1))
        else:
            xs = _moe_combine(ys, d0, d1, xs.reshape(ntok, d), mt, 5, bpb)
        xs = xs.reshape(bsz, tlen, d)

    return xs[:, CTX_LEN:]
```
